```python
import math
import jax, jax.numpy as jnp
from jax import lax
import numpy as np

D_MODEL = 1024
BATCH = 4
SEQ = 4096
DEPTH = 2

GRID_W = 64
CTX_LEN = 256
EPS = 1e-6
ROPE_BASE = 10000.0

CONV_CH = 256
CONV_K = 31
RET_HEADS = 4
RET_DK = 64
RET_DV = 64
RET_CHUNK = 64
MLA_HEADS = 8
MLA_Q_RANK = 256
MLA_KV_RANK = 128
MLA_NOPE = 64
MLA_ROPE = 32
MLA_V = 64
D_MIX = CONV_CH + RET_HEADS * RET_DV + MLA_HEADS * MLA_V
PROJ_SIZES = (2 * CONV_CH, RET_HEADS * RET_DK, RET_HEADS * RET_DK, RET_HEADS * RET_DV, RET_HEADS * RET_DV,
              MLA_Q_RANK, MLA_KV_RANK, MLA_ROPE)
D_IN = 2 * CONV_CH + 2 * RET_HEADS * RET_DK + 2 * RET_HEADS * RET_DV + MLA_Q_RANK + MLA_KV_RANK + MLA_ROPE
D_FF = ((8 * D_MODEL // 3 + 255) // 256) * 256
ATTN_BLOCK = 128

kernel_name = "hybrid_conv_retention_mla_dit_block"


def rmsnorm(x, g):
    xf = x.astype(jnp.float32)
    y = xf * lax.rsqrt(jnp.mean(xf * xf, axis=-1, keepdims=True) + EPS)
    return (y * g.astype(jnp.float32)).astype(x.dtype)


def layernorm(x, g, b):
    xf = x.astype(jnp.float32)
    mu = jnp.mean(xf, axis=-1, keepdims=True)
    var = jnp.mean(jnp.square(xf - mu), axis=-1, keepdims=True)
    y = (xf - mu) * lax.rsqrt(var + EPS)
    return (y * g.astype(jnp.float32) + b.astype(jnp.float32)).astype(x.dtype)


def modulate(h, shift, scale):
    return h * (1.0 + scale[:, None, :]) + shift[:, None, :]


def _rotate_half_axis(x, pos):
    f = x.shape[-1] // 2
    inv = ROPE_BASE ** (-jnp.arange(f, dtype=jnp.float32) / f)
    ang = pos.astype(jnp.float32)[:, None] * inv[None, :]
    cos = jnp.cos(ang)[None, :, None, :]
    sin = jnp.sin(ang)[None, :, None, :]
    xf = x.astype(jnp.float32)
    x1, x2 = xf[..., :f], xf[..., f:]
    return jnp.concatenate([x1 * cos - x2 * sin, x1 * sin + x2 * cos], axis=-1).astype(x.dtype)


def rope_2d(x, row, col):
    h = x.shape[-1] // 2
    return jnp.concatenate([_rotate_half_axis(x[..., :h], row), _rotate_half_axis(x[..., h:], col)], axis=-1)


def split_proj(p):
    B, T, _ = p.shape
    parts = []
    off = 0
    for s in PROJ_SIZES:
        parts.append(p[..., off:off + s])
        off += s
    a, q, k, v, g, cq, ckv, kr = parts
    q = q.reshape(B, T, RET_HEADS, RET_DK)
    k = k.reshape(B, T, RET_HEADS, RET_DK) * (RET_DK ** -0.5)
    v = v.reshape(B, T, RET_HEADS, RET_DV)
    return a, q, k, v, g, cq, ckv, kr


def conv_module(a, w_dw, b_dw, ln_g, ln_b):
    u, gt = jnp.split(a, 2, axis=-1)
    y = u * jax.nn.sigmoid(gt)
    y = lax.conv_general_dilated(y, w_dw[:, None, :], window_strides=(1,),
                                 padding=[(CONV_K // 2, CONV_K // 2)],
                                 dimension_numbers=('NWC', 'WIO', 'NWC'),
                                 feature_group_count=CONV_CH) + b_dw
    return jax.nn.silu(layernorm(y, ln_g, ln_b))


def retention_dir(q, k, v, log_g, s0, strict):
    B, T, H, DK = q.shape
    DV = v.shape[-1]
    C = RET_CHUNK
    N = T // C

    def chunks(t):
        return t.astype(jnp.float32).reshape(B, N, C, H, t.shape[-1]).transpose(1, 0, 3, 2, 4)

    qc, kc, vc = chunks(q), chunks(k), chunks(v)
    lg = log_g.astype(jnp.float32)[:, None]
    idx = jnp.arange(C, dtype=jnp.float32)
    diff = idx[:, None] - idx[None, :]
    mask = (diff > 0) if strict else (diff >= 0)
    dmat = jnp.where(mask[None], jnp.exp(lg[:, :, None] * jnp.maximum(diff, 0.0)[None]), 0.0)
    xi = jnp.exp(lg * (idx + 1.0))[:, :, None]
    zeta = jnp.exp(lg * (C - 1.0 - idx))[:, :, None]
    g_chunk = jnp.exp(lg * C)[:, :, None]

    def step(s, inp):
        qi, ki, vi = inp
        inner = jnp.einsum('bhcd,bhmd->bhcm', qi, ki) * dmat
        o = jnp.einsum('bhcm,bhmv->bhcv', inner, vi) + jnp.einsum('bhcd,bhdv->bhcv', qi, s) * xi
        s = s * g_chunk + jnp.einsum('bhmd,bhmv->bhdv', ki * zeta, vi)
        return s, o

    s, o = lax.scan(step, s0, (qc, kc, vc))
    o = o.transpose(1, 0, 3, 2, 4).reshape(B, T, H, DV)
    return o, s


def retention_bidir(q, k, v, log_g, s_fwd, s_bwd):
    o_f, sf = retention_dir(q, k, v, log_g[0], s_fwd, False)
    o_b, sb = retention_dir(q[:, ::-1], k[:, ::-1], v[:, ::-1], log_g[1], s_bwd, True)
    return o_f + o_b[:, ::-1], sf, sb


def retention_final_states(k, v, log_g):
    T = k.shape[1]
    pos = jnp.arange(T, dtype=jnp.float32)
    lg = log_g.astype(jnp.float32)
    wf = jnp.exp(lg[0][None, :] * (T - 1.0 - pos)[:, None])
    wb = jnp.exp(lg[1][None, :] * pos[:, None])
    kf = k.astype(jnp.float32)
    vf = v.astype(jnp.float32)
    sf = jnp.einsum('bthd,th,bthv->bhdv', kf, wf, vf)
    sb = jnp.einsum('bthd,th,bthv->bhdv', kf, wb, vf)
    return sf, sb


def head_groupnorm(o, g):
    B, T, H, DV = o.shape
    mu = jnp.mean(o, axis=-1, keepdims=True)
    var = jnp.mean(jnp.square(o - mu), axis=-1, keepdims=True)
    y = ((o - mu) * lax.rsqrt(var + EPS)).reshape(B, T, H * DV)
    return y * g.astype(jnp.float32)


def mla_q(cq, q_norm_g, w_uq, row, col):
    B, T, _ = cq.shape
    q = (rmsnorm(cq, q_norm_g) @ w_uq).reshape(B, T, MLA_HEADS, MLA_NOPE + MLA_ROPE)
    q_nope, q_rope = q[..., :MLA_NOPE], q[..., MLA_NOPE:]
    if row is not None:
        q_rope = rope_2d(q_rope, row, col)
    return jnp.concatenate([q_nope, q_rope], axis=-1)


def mla_kv(ckv, kr, kv_norm_g, w_ukv, row, col):
    B, T, _ = ckv.shape
    kv = (rmsnorm(ckv, kv_norm_g) @ w_ukv).reshape(B, T, MLA_HEADS, MLA_NOPE + MLA_V)
    k_nope, v = kv[..., :MLA_NOPE], kv[..., MLA_NOPE:]
    kr = kr[:, :, None, :]
    if row is not None:
        kr = rope_2d(kr, row, col)
    k = jnp.concatenate([k_nope, jnp.broadcast_to(kr, (B, T, MLA_HEADS, MLA_ROPE))], axis=-1)
    return k, v


def block_attention(q, k, v):
    B, T, H, dq = q.shape
    dv = v.shape[-1]
    nb = T // ATTN_BLOCK
    scale = dq ** -0.5
    qb = q.reshape(B, nb, ATTN_BLOCK, H, dq).transpose(1, 0, 2, 3, 4)

    def one(qi):
        s = jnp.einsum('bqhd,bkhd->bhqk', qi, k).astype(jnp.float32) * scale
        p = jax.nn.softmax(s, axis=-1).astype(v.dtype)
        return jnp.einsum('bhqk,bkhd->bqhd', p, v)

    o = lax.map(one, qb)
    return o.transpose(1, 0, 2, 3, 4).reshape(B, T, H * dv)


def mixer_out(a, o_ret, g, att, conv_w, conv_b, conv_ln_g, conv_ln_b, ret_gn_g, w_out):
    y_conv = conv_module(a, conv_w, conv_b, conv_ln_g, conv_ln_b)
    y_ret = head_groupnorm(o_ret, ret_gn_g).astype(g.dtype) * jax.nn.silu(g)
    return jnp.concatenate([y_conv, y_ret, att], axis=-1) @ w_out


def swiglu(h, w1, w2):
    u, gt = jnp.split(h @ w1, 2, axis=-1)
    return (jax.nn.silu(gt) * u) @ w2


def setup_inputs(seed: int = 0) -> dict:
    key = jax.random.key(seed)
    ks = iter(jax.random.split(key, 32))
    L = DEPTH

    def nrm(shape, scale):
        return jax.random.normal(next(ks), shape, jnp.float32) * scale

    def gain(n):
        return 1.0 + nrm((L, n), 0.02)

    base = jnp.log1p(-jnp.power(2.0, -5.0 - jnp.arange(RET_HEADS, dtype=jnp.float32)))
    return {
        "x": nrm((BATCH, SEQ, D_MODEL), 1.0),
        "c": nrm((BATCH, D_MODEL), 1.0),
        "ctx": nrm((BATCH, CTX_LEN, D_MODEL), 1.0),
        "c_ctx": nrm((D_MODEL,), 1.0),
        "mod_w": nrm((L, D_MODEL, 6 * D_MODEL), 0.5 * D_MODEL ** -0.5),
        "mod_b": nrm((L, 6 * D_MODEL), 0.02),
        "pre1_g": gain(D_MODEL),
        "post1_g": gain(D_MODEL),
        "pre2_g": gain(D_MODEL),
        "post2_g": gain(D_MODEL),
        "w_in": nrm((L, D_MODEL, D_IN), D_MODEL ** -0.5),
        "conv_w": nrm((L, CONV_K, CONV_CH), CONV_K ** -0.5),
        "conv_b": nrm((L, CONV_CH), 0.02),
        "conv_ln_g": gain(CONV_CH),
        "conv_ln_b": nrm((L, CONV_CH), 0.02),
        "ret_log_decay": base[None, None, :] * jnp.exp(nrm((L, 2, RET_HEADS), 0.1)),
        "ret_gn_g": gain(RET_HEADS * RET_DV),
        "mla_q_norm_g": gain(MLA_Q_RANK),
        "mla_w_uq": nrm((L, MLA_Q_RANK, MLA_HEADS * (MLA_NOPE + MLA_ROPE)), MLA_Q_RANK ** -0.5),
        "mla_kv_norm_g": gain(MLA_KV_RANK),
        "mla_w_ukv": nrm((L, MLA_KV_RANK, MLA_HEADS * (MLA_NOPE + MLA_V)), MLA_KV_RANK ** -0.5),
        "w_out": nrm((L, D_MIX, D_MODEL), D_MIX ** -0.5),
        "ffn_w_in": nrm((L, D_MODEL, 2 * D_FF), D_MODEL ** -0.5),
        "ffn_w_out": nrm((L, D_FF, D_MODEL), D_FF ** -0.5),
    }


def reference(x, c, ctx, c_ctx, mod_w, mod_b, pre1_g, post1_g, pre2_g, post2_g, w_in, conv_w, conv_b,
              conv_ln_g, conv_ln_b, ret_log_decay, ret_gn_g, mla_q_norm_g, mla_w_uq, mla_kv_norm_g, mla_w_ukv,
              w_out, ffn_w_in, ffn_w_out):
    B, T, _ = x.shape
    ROWS = T // GRID_W
    row = jnp.repeat(jnp.arange(ROWS, dtype=jnp.int32), GRID_W)
    col = jnp.tile(jnp.arange(GRID_W, dtype=jnp.int32), ROWS)
    xc = ctx
    for l in range(DEPTH):
        last = l == DEPTH - 1
        sh1, sc1, g1, sh2, sc2, g2 = jnp.split(jax.nn.silu(c) @ mod_w[l] + mod_b[l], 6, axis=-1)
        csh1, csc1, cg1, csh2, csc2, cg2 = jnp.split(jax.nn.silu(c_ctx)[None] @ mod_w[l] + mod_b[l], 6, axis=-1)

        pl = modulate(rmsnorm(x, pre1_g[l]), sh1, sc1) @ w_in[l]
        pc = modulate(rmsnorm(xc, pre1_g[l]), csh1, csc1) @ w_in[l]
        aL, qL, kL, vL, gL, cqL, ckvL, krL = split_proj(pl)
        aC, qC, kC, vC, gC, cqC, ckvC, krC = split_proj(pc)

        lgd = ret_log_decay[l]
        if last:
            sf, sb = retention_final_states(kC, vC, lgd)
        else:
            zeros = jnp.zeros((B, RET_HEADS, RET_DK, RET_DV), jnp.float32)
            oC_ret, sf, sb = retention_bidir(qC, kC, vC, lgd, zeros, zeros)
        oL_ret, _, _ = retention_bidir(rope_2d(qL, row, col), rope_2d(kL, row, col), vL, lgd, sf, sb)

        kmC, vmC = mla_kv(ckvC, krC, mla_kv_norm_g[l], mla_w_ukv[l], None, None)
        kmL, vmL = mla_kv(ckvL, krL, mla_kv_norm_g[l], mla_w_ukv[l], row, col)
        qmL = mla_q(cqL, mla_q_norm_g[l], mla_w_uq[l], row, col)
        attL = block_attention(qmL, jnp.concatenate([kmC, kmL], axis=1), jnp.concatenate([vmC, vmL], axis=1))

        yL = mixer_out(aL, oL_ret, gL, attL, conv_w[l], conv_b[l], conv_ln_g[l], conv_ln_b[l], ret_gn_g[l], w_out[l])
        x = x + g1[:, None, :] * rmsnorm(yL, post1_g[l])
        if not last:
            qmC = mla_q(cqC, mla_q_norm_g[l], mla_w_uq[l], None, None)
            attC = block_attention(qmC, kmC, vmC)
            yC = mixer_out(aC, oC_ret, gC, attC, conv_w[l], conv_b[l], conv_ln_g[l], conv_ln_b[l], ret_gn_g[l], w_out[l])
            xc = xc + cg1[:, None, :] * rmsnorm(yC, post1_g[l])

        hL = swiglu(modulate(rmsnorm(x, pre2_g[l]), sh2, sc2), ffn_w_in[l], ffn_w_out[l])
        x = x + g2[:, None, :] * rmsnorm(hL, post2_g[l])
        if not last:
            hC = swiglu(modulate(rmsnorm(xc, pre2_g[l]), csh2, csc2), ffn_w_in[l], ffn_w_out[l])
            xc = xc + cg2[:, None, :] * rmsnorm(hC, post2_g[l])
    return x
```

```python
import functools

import jax
import jax.numpy as jnp
from jax import lax
from jax.experimental import pallas as pl
from jax.experimental.pallas import tpu as pltpu

F32 = jnp.float32
BF16 = jnp.bfloat16

D_MODEL = 1024
DEPTH = 2
GRID_W = 64
EPS = 1e-6
ROPE_BASE = 10000.0
CONV_CH = 256
CONV_K = 31
RET_HEADS = 4
RET_DK = 64
RET_DV = 64
MLA_HEADS = 8
MLA_Q_RANK = 256
MLA_KV_RANK = 128
MLA_NOPE = 64
MLA_ROPE = 32
MLA_V = 64
D_FF = 2816

LANES = 128
HALO = 16
ROW_TILE = 256
RET_CHUNK = 256
ATT_TQ = 256
ATT_TK = 512
FF_CHUNK = 1408
VMEM_LIMIT = 56 * 1024 * 1024

_OFF_A = 0
_OFF_Q = 512
_OFF_QR = 768
_OFF_K = 1024
_OFF_KR = 1280
_OFF_V = 1536
_OFF_G = 1792
_OFF_CQ = 2048
_OFF_CKV = 2304
_OFF_KRP = 2432
_OFF_KRPR = 2560
D_EXT = 2688
MQ_W = MLA_HEADS * LANES


def _sigmoid(x):
    return 1.0 / (1.0 + jnp.exp(-x))


def _silu(x):
    return x * _sigmoid(x)


def _params(*sem):
    return pltpu.CompilerParams(dimension_semantics=sem, vmem_limit_bytes=VMEM_LIMIT)


def _resident(shape):
    n = len(shape)
    return pl.BlockSpec(shape, lambda *_: (0,) * n, pipeline_mode=pl.Buffered(1))


def _mod_kernel(cv_ref, w_ref, b_ref, o_ref):
    s = _silu(cv_ref[...])
    o_ref[0] = jnp.dot(s.astype(BF16), w_ref[0].astype(BF16), preferred_element_type=F32) + b_ref[0]


def _modulation(cv, mod_w, mod_b):
    L, D, N = mod_w.shape
    tn = 1536
    return pl.pallas_call(
        _mod_kernel,
        out_shape=jax.ShapeDtypeStruct((L, 8, N), F32),
        grid=(L, N // tn),
        in_specs=[pl.BlockSpec((8, D), lambda l, j: (0, 0)),
                  pl.BlockSpec((1, D, tn), lambda l, j: (l, 0, j)),
                  pl.BlockSpec((1, 1, tn), lambda l, j: (l, 0, j))],
        out_specs=pl.BlockSpec((1, 8, tn), lambda l, j: (l, 0, j)),
        compiler_params=_params("arbitrary", "arbitrary"),
        name="modulation",
    )(cv, mod_w, mod_b.reshape(L, 1, N))


def _inproj_kernel(x_ref, sh_ref, sc_ref, g_ref, w_ref, qng_ref, wuq_ref, kvg_ref, wukv_ref,
                   cosr_ref, sinr_ref, cosm_ref, sinm_ref,
                   a_ref, rq_ref, rk_ref, rv_ref, rg_ref, mq_ref, mk_ref, mve_ref, mvo_ref):
    x = x_ref[0]
    h = x * lax.rsqrt(jnp.mean(x * x, axis=-1, keepdims=True) + EPS) * g_ref[...]
    h = h * (1.0 + sc_ref[0]) + sh_ref[0]
    p = jnp.dot(h.astype(BF16), w_ref[...], preferred_element_type=F32)

    a_ref[0] = p[:, _OFF_A:_OFF_A + CONV_CH] * _sigmoid(p[:, _OFF_A + CONV_CH:_OFF_A + 2 * CONV_CH])

    cosr = cosr_ref[...]
    sinr = sinr_ref[...]
    for i in range(2):
        lo, hi = i * LANES, (i + 1) * LANES
        rq_ref[0, :, lo:hi] = (p[:, _OFF_Q + lo:_OFF_Q + hi] * cosr
                               + p[:, _OFF_QR + lo:_OFF_QR + hi] * sinr).astype(BF16)
        rk_ref[0, :, lo:hi] = (p[:, _OFF_K + lo:_OFF_K + hi] * cosr
                               + p[:, _OFF_KR + lo:_OFF_KR + hi] * sinr).astype(BF16)
    rv_ref[0] = p[:, _OFF_V:_OFF_V + 256].astype(BF16)
    rg_ref[0] = p[:, _OFF_G:_OFF_G + 256]

    cosm = cosm_ref[...]
    sinm = sinm_ref[...]
    cq = p[:, _OFF_CQ:_OFF_CQ + MLA_Q_RANK]
    qn = cq * lax.rsqrt(jnp.mean(cq * cq, axis=-1, keepdims=True) + EPS) * qng_ref[...]
    qq = jnp.dot(qn.astype(BF16), wuq_ref[...], preferred_element_type=F32)
    scale = float((MLA_NOPE + MLA_ROPE) ** -0.5)
    ckv = p[:, _OFF_CKV:_OFF_CKV + MLA_KV_RANK]
    kvn = ckv * lax.rsqrt(jnp.mean(ckv * ckv, axis=-1, keepdims=True) + EPS) * kvg_ref[...]
    kv = jnp.dot(kvn.astype(BF16), wukv_ref[...], preferred_element_type=F32)
    kr_slot = p[:, _OFF_KRP:_OFF_KRP + LANES] * cosm + p[:, _OFF_KRPR:_OFF_KRPR + LANES] * sinm
    for hh in range(MLA_HEADS):
        lo, hi = hh * LANES, (hh + 1) * LANES
        mq_ref[0, :, lo:hi] = ((qq[:, lo:hi] * cosm + qq[:, MQ_W + lo:MQ_W + hi] * sinm) * scale).astype(BF16)
        mk_ref[0, :, lo:hi] = (kv[:, lo:hi] + kr_slot).astype(BF16)
    v = kv[:, MQ_W:]
    lane = lax.broadcasted_iota(jnp.int32, v.shape, 1)
    even = (lane % LANES) < MLA_V
    mve_ref[0] = jnp.where(even, v, 0.0).astype(BF16)
    mvo_ref[0] = jnp.where(even, 0.0, v).astype(BF16)


def _inproj(x, sh, sc, g, w_ext, qng, wuq, kvg, wukv, cosr, sinr, cosm, sinm):
    B, L, D = x.shape
    tm = ROW_TILE
    row = lambda w: pl.BlockSpec((1, tm, w), lambda b, i: (b, i, 0))
    vec = lambda w: pl.BlockSpec((1, 1, w), lambda b, i: (b, 0, 0))
    tab = pl.BlockSpec((tm, LANES), lambda b, i: (i, 0))
    outs = [(CONV_CH, F32), (256, BF16), (256, BF16), (256, BF16), (256, F32),
            (MQ_W, BF16), (MQ_W, BF16), (512, BF16), (512, BF16)]
    return pl.pallas_call(
        _inproj_kernel,
        out_shape=[jax.ShapeDtypeStruct((B, L, w), dt) for w, dt in outs],
        grid=(B, L // tm),
        in_specs=[row(D), vec(D), vec(D), _resident((1, D)), _resident(w_ext.shape),
                  _resident((1, MLA_Q_RANK)), _resident(wuq.shape),
                  _resident((1, MLA_KV_RANK)), _resident(wukv.shape), tab, tab, tab, tab],
        out_specs=[row(w) for w, _ in outs],
        compiler_params=_params("arbitrary", "arbitrary"),
        name="inproj",
    )(x, sh, sc, g, w_ext, qng, wuq, kvg, wukv, cosr, sinr, cosm, sinm)


def _ret_kernel(lgd_ref, q_ref, k_ref, v_ref, g_ref, gng_ref, lgf_ref, lgb_ref, sf0_ref, sb0_ref,
                y_ref, sfo_ref, sbo_ref, w_scr, kvf_scr, kvb_scr, sfs_scr, sbs_scr, *, n_chunks):
    C = RET_CHUNK
    N = n_chunks

    @pl.when(pl.program_id(0) == 0)
    def _():
        t = lax.broadcasted_iota(jnp.int32, (C, C), 0)
        m = lax.broadcasted_iota(jnp.int32, (C, C), 1)
        d = (t - m).astype(F32)
        for hh in range(RET_HEADS):
            w_scr[hh] = jnp.exp(jnp.where(d >= 0.0, lgd_ref[0, hh] * d, lgd_ref[1, hh] * (-d)))

    j = lax.broadcasted_iota(jnp.int32, (C, LANES), 0).astype(F32)
    lane = lax.broadcasted_iota(jnp.int32, (C, LANES), 1)
    first = lane < RET_DV
    r128 = lax.broadcasted_iota(jnp.int32, (LANES, LANES), 0)
    c128 = lax.broadcasted_iota(jnp.int32, (LANES, LANES), 1)
    blockdiag = (r128 < RET_DK) == (c128 < RET_DV)

    for pi in range(RET_HEADS // 2):
        lo, hi = pi * LANES, (pi + 1) * LANES
        lgf = lgf_ref[:, lo:hi]
        lgb = lgb_ref[:, lo:hi]
        zeta_f = jnp.exp(lgf * (C - 1.0 - j))
        zeta_b = jnp.exp(lgb * j)
        xi_f = jnp.exp(lgf * (j + 1.0))
        xi_b = jnp.exp(lgb * (C - j))
        gc_f = jnp.exp(lgf * float(C))
        gc_b = jnp.exp(lgb * float(C))

        def chunk_sums(i, carry):
            rows = pl.ds(pl.multiple_of(i * C, C), C)
            kf = k_ref[0, rows, lo:hi].astype(F32)
            vv = v_ref[0, rows, lo:hi]
            kzf = (kf * zeta_f).T.astype(BF16)
            kzb = (kf * zeta_b).T.astype(BF16)
            kvf_scr[i] = jnp.where(blockdiag, jnp.dot(kzf, vv, preferred_element_type=F32), 0.0)
            kvb_scr[i] = jnp.where(blockdiag, jnp.dot(kzb, vv, preferred_element_type=F32), 0.0)
            return carry

        lax.fori_loop(0, N, chunk_sums, 0)

        def fwd_scan(i, s):
            sfs_scr[i] = s
            return s * gc_f + kvf_scr[i]

        sfo_ref[0, pi] = lax.fori_loop(0, N, fwd_scan, sf0_ref[0, pi])

        def bwd_scan(ii, s):
            i = N - 1 - ii
            sbs_scr[i] = s
            return s * gc_b + kvb_scr[i]

        sbo_ref[0, pi] = lax.fori_loop(0, N, bwd_scan, sb0_ref[0, pi])

        gng = gng_ref[:, lo:hi]

        def chunk_out(i, carry):
            rows = pl.ds(pl.multiple_of(i * C, C), C)
            q = q_ref[0, rows, lo:hi]
            k = k_ref[0, rows, lo:hi]
            vv = v_ref[0, rows, lo:hi]
            zero = jnp.zeros_like(q)
            dn = (((1,), (1,)), ((), ()))
            s0 = lax.dot_general(jnp.where(first, q, zero), k, dn, preferred_element_type=F32)
            s1 = lax.dot_general(jnp.where(first, zero, q), k, dn, preferred_element_type=F32)
            p0 = (s0 * w_scr[2 * pi]).astype(BF16)
            p1 = (s1 * w_scr[2 * pi + 1]).astype(BF16)
            o = jnp.dot(p0, jnp.where(first, vv, zero), preferred_element_type=F32)
            o = o + jnp.dot(p1, jnp.where(first, zero, vv), preferred_element_type=F32)
            o = o + jnp.dot(q, sfs_scr[i].astype(BF16), preferred_element_type=F32) * xi_f
            o = o + jnp.dot(q, sbs_scr[i].astype(BF16), preferred_element_type=F32) * xi_b
            inv = 1.0 / RET_DV
            tot = jnp.sum(o, axis=-1, keepdims=True)
            m0 = jnp.sum(jnp.where(first, o, 0.0), axis=-1, keepdims=True)
            mu = jnp.where(first, m0, tot - m0) * inv
            dlt = o - mu
            d2 = dlt * dlt
            tot2 = jnp.sum(d2, axis=-1, keepdims=True)
            v0 = jnp.sum(jnp.where(first, d2, 0.0), axis=-1, keepdims=True)
            var = jnp.where(first, v0, tot2 - v0) * inv
            y = dlt * lax.rsqrt(var + EPS) * gng
            y_ref[0, rows, lo:hi] = (y * _silu(g_ref[0, rows, lo:hi])).astype(BF16)
            return carry

        lax.fori_loop(0, N, chunk_out, 0)


def _retention(lgd, q, k, v, g, gng, lgf, lgb, sf0, sb0):
    B, L, W = q.shape
    C = RET_CHUNK
    N = L // C
    P = RET_HEADS // 2
    seq = lambda: pl.BlockSpec((1, L, W), lambda b: (b, 0, 0))
    st = lambda: pl.BlockSpec((1, P, LANES, LANES), lambda b: (b, 0, 0, 0))
    st_shape = jax.ShapeDtypeStruct((B, P, LANES, LANES), F32)
    return pl.pallas_call(
        functools.partial(_ret_kernel, n_chunks=N),
        out_shape=[jax.ShapeDtypeStruct((B, L, W), BF16), st_shape, st_shape],
        grid=(B,),
        in_specs=[pl.BlockSpec(memory_space=pltpu.SMEM), seq(), seq(), seq(), seq(),
                  pl.BlockSpec((1, W), lambda b: (0, 0)), pl.BlockSpec((1, W), lambda b: (0, 0)),
                  pl.BlockSpec((1, W), lambda b: (0, 0)), st(), st()],
        out_specs=[seq(), st(), st()],
        scratch_shapes=[pltpu.VMEM((RET_HEADS, C, C), F32),
                        pltpu.VMEM((N, LANES, LANES), F32), pltpu.VMEM((N, LANES, LANES), F32),
                        pltpu.VMEM((N, LANES, LANES), F32), pltpu.VMEM((N, LANES, LANES), F32)],
        compiler_params=_params("arbitrary"),
        name="retention",
    )(lgd, q, k, v, g, gng, lgf, lgb, sf0, sb0)


def _attn_kernel(*refs, lengths):
    q_ref = refs[0]
    src = refs[1:1 + 3 * len(lengths)]
    o_ref = refs[1 + 3 * len(lengths)]
    tq = q_ref.shape[1]
    lane = lax.broadcasted_iota(jnp.int32, (tq, LANES), 1)
    first = lane < MLA_V
    dn = (((1,), (1,)), ((), ()))

    for pj in range(MLA_HEADS // 2):
        q0 = q_ref[0, :, (2 * pj) * LANES:(2 * pj + 1) * LANES]
        q1 = q_ref[0, :, (2 * pj + 1) * LANES:(2 * pj + 2) * LANES]
        vlo, vhi = pj * LANES, (pj + 1) * LANES

        def step(k_ref, ve_ref, vo_ref, rows, carry):
            m0, l0, m1, l1, acc = carry
            k0 = k_ref[0, rows, (2 * pj) * LANES:(2 * pj + 1) * LANES]
            k1 = k_ref[0, rows, (2 * pj + 1) * LANES:(2 * pj + 2) * LANES]
            s0 = lax.dot_general(q0, k0, dn, preferred_element_type=F32)
            s1 = lax.dot_general(q1, k1, dn, preferred_element_type=F32)
            n0 = jnp.maximum(m0, jnp.max(s0, axis=-1, keepdims=True))
            n1 = jnp.maximum(m1, jnp.max(s1, axis=-1, keepdims=True))
            a0 = jnp.exp(m0 - n0)
            a1 = jnp.exp(m1 - n1)
            p0 = jnp.exp(s0 - n0)
            p1 = jnp.exp(s1 - n1)
            l0 = a0 * l0 + jnp.sum(p0, axis=-1, keepdims=True)
            l1 = a1 * l1 + jnp.sum(p1, axis=-1, keepdims=True)
            acc = acc * jnp.where(first, a0, a1)
            acc = acc + jnp.dot(p0.astype(BF16), ve_ref[0, rows, vlo:vhi], preferred_element_type=F32)
            acc = acc + jnp.dot(p1.astype(BF16), vo_ref[0, rows, vlo:vhi], preferred_element_type=F32)
            return n0, l0, n1, l1, acc

        neg = jnp.full((tq, 1), -1e30, F32)
        zero = jnp.zeros((tq, 1), F32)
        carry = (neg, zero, neg, zero, jnp.zeros((tq, LANES), F32))
        for si, length in enumerate(lengths):
            k_ref, ve_ref, vo_ref = src[3 * si:3 * si + 3]
            tk = min(ATT_TK, length)
            n = length // tk
            if n == 1:
                carry = step(k_ref, ve_ref, vo_ref, pl.ds(0, tk), carry)
            else:
                carry = lax.fori_loop(
                    0, n,
                    lambda c, cr: step(k_ref, ve_ref, vo_ref, pl.ds(pl.multiple_of(c * tk, tk), tk), cr),
                    carry)
        _, l0, _, l1, acc = carry
        o_ref[0, :, vlo:vhi] = (acc / jnp.where(first, l0, l1)).astype(BF16)


def _attention(q, sources):
    B, L, _ = q.shape
    tq = ATT_TQ
    lengths = tuple(s[0].shape[1] for s in sources)
    in_specs = [pl.BlockSpec((1, tq, MQ_W), lambda b, i: (b, i, 0))]
    args = [q]
    for k, ve, vo in sources:
        S = k.shape[1]
        in_specs += [pl.BlockSpec((1, S, MQ_W), lambda b, i: (b, 0, 0)),
                     pl.BlockSpec((1, S, 512), lambda b, i: (b, 0, 0)),
                     pl.BlockSpec((1, S, 512), lambda b, i: (b, 0, 0))]
        args += [k, ve, vo]
    return pl.pallas_call(
        functools.partial(_attn_kernel, lengths=lengths),
        out_shape=jax.ShapeDtypeStruct((B, L, MLA_HEADS * MLA_V), BF16),
        grid=(B, L // tq),
        in_specs=in_specs,
        out_specs=pl.BlockSpec((1, tq, MLA_HEADS * MLA_V), lambda b, i: (b, i, 0)),
        compiler_params=_params("arbitrary", "arbitrary"),
        name="attention",
    )(*args)


def _mixer_kernel(a_ref, ap_ref, an_ref, cw_ref, cb_ref, lg_ref, lb_ref, yr_ref, at_ref, wo_ref,
                  pg_ref, gate_ref, x_ref, o_ref, win_scr, yc_scr):
    tm = a_ref.shape[1]
    i = pl.program_id(1)
    n = pl.num_programs(1)
    win_scr[0:HALO] = jnp.where(i > 0, ap_ref[0], 0.0)
    win_scr[HALO:HALO + tm] = a_ref[0]
    win_scr[HALO + tm:HALO + tm + HALO] = jnp.where(i < n - 1, an_ref[0], 0.0)
    sub = 64
    off = HALO - CONV_K // 2
    for r0 in range(0, tm, sub):
        acc = jnp.zeros((sub, CONV_CH), F32) + cb_ref[...]
        for t in range(CONV_K):
            acc = acc + cw_ref[t:t + 1, :] * win_scr[r0 + off + t:r0 + off + t + sub, :]
        mu = jnp.mean(acc, axis=-1, keepdims=True)
        d = acc - mu
        var = jnp.mean(d * d, axis=-1, keepdims=True)
        yc = d * lax.rsqrt(var + EPS) * lg_ref[...] + lb_ref[...]
        yc_scr[r0:r0 + sub, :] = _silu(yc).astype(BF16)
    y = jnp.dot(yc_scr[...], wo_ref[0:256, :], preferred_element_type=F32)
    y = y + jnp.dot(yr_ref[0], wo_ref[256:512, :], preferred_element_type=F32)
    y = y + jnp.dot(at_ref[0], wo_ref[512:1024, :], preferred_element_type=F32)
    r = y * lax.rsqrt(jnp.mean(y * y, axis=-1, keepdims=True) + EPS) * pg_ref[...]
    o_ref[0] = x_ref[0] + gate_ref[0] * r


def _mixer(a, cw, cb, lg, lb, yr, att, wo, pg, gate, x):
    B, L, D = x.shape
    tm = ROW_TILE
    hb = tm // HALO
    nh = L // HALO
    row = lambda w: pl.BlockSpec((1, tm, w), lambda b, i: (b, i, 0))
    return pl.pallas_call(
        _mixer_kernel,
        out_shape=jax.ShapeDtypeStruct((B, L, D), F32),
        grid=(B, L // tm),
        in_specs=[row(CONV_CH),
                  pl.BlockSpec((1, HALO, CONV_CH), lambda b, i: (b, jnp.maximum(i * hb - 1, 0), 0)),
                  pl.BlockSpec((1, HALO, CONV_CH), lambda b, i: (b, jnp.minimum((i + 1) * hb, nh - 1), 0)),
                  _resident(cw.shape), _resident((1, CONV_CH)), _resident((1, CONV_CH)),
                  _resident((1, CONV_CH)), row(256), row(512), _resident(wo.shape),
                  _resident((1, D)), pl.BlockSpec((1, 1, D), lambda b, i: (b, 0, 0)), row(D)],
        out_specs=row(D),
        scratch_shapes=[pltpu.VMEM((tm + 2 * HALO, CONV_CH), F32), pltpu.VMEM((tm, CONV_CH), BF16)],
        compiler_params=_params("arbitrary", "arbitrary"),
        name="mixer_out",
    )(a, a, a, cw, cb, lg, lb, yr, att, wo, pg, gate, x)


def _ffn_kernel(x_ref, sh_ref, sc_ref, g_ref, w1_ref, w2_ref, pg_ref, gate_ref, o_ref):
    x = x_ref[0]
    h = x * lax.rsqrt(jnp.mean(x * x, axis=-1, keepdims=True) + EPS) * g_ref[...]
    h = (h * (1.0 + sc_ref[0]) + sh_ref[0]).astype(BF16)
    y = jnp.zeros(x.shape, F32)
    for c in range(D_FF // FF_CHUNK):
        lo, hi = c * FF_CHUNK, (c + 1) * FF_CHUNK
        u = jnp.dot(h, w1_ref[:, lo:hi], preferred_element_type=F32)
        gt = jnp.dot(h, w1_ref[:, D_FF + lo:D_FF + hi], preferred_element_type=F32)
        y = y + jnp.dot((_silu(gt) * u).astype(BF16), w2_ref[lo:hi, :], preferred_element_type=F32)
    r = y * lax.rsqrt(jnp.mean(y * y, axis=-1, keepdims=True) + EPS) * pg_ref[...]
    o_ref[0] = x + gate_ref[0] * r


def _ffn(x, sh, sc, g, w1, w2, pg, gate):
    B, L, D = x.shape
    tm = ROW_TILE
    row = pl.BlockSpec((1, tm, D), lambda b, i: (b, i, 0))
    vec = pl.BlockSpec((1, 1, D), lambda b, i: (b, 0, 0))
    return pl.pallas_call(
        _ffn_kernel,
        out_shape=jax.ShapeDtypeStruct((B, L, D), F32),
        grid=(B, L // tm),
        in_specs=[row, vec, vec, _resident((1, D)), _resident(w1.shape), _resident(w2.shape),
                  _resident((1, D)), vec],
        out_specs=row,
        compiler_params=_params("arbitrary", "arbitrary"),
        name="ffn",
    )(x, sh, sc, g, w1, w2, pg, gate)


def _rot_cols(w, f):
    lead = w.shape[:-1]
    wr = w.reshape(lead + (-1, 2, f))
    return jnp.stack([-wr[..., 1, :], wr[..., 0, :]], axis=-2).reshape(w.shape)


def _rope_tables(T):
    t = jnp.arange(T, dtype=jnp.int32)
    rowp = (t // GRID_W).astype(F32)[:, None]
    colp = (t % GRID_W).astype(F32)[:, None]

    def half(f):
        inv = ROPE_BASE ** (-jnp.arange(f, dtype=F32) / f)
        ar, ac = rowp * inv[None, :], colp * inv[None, :]
        ang = jnp.concatenate([ar, ar, ac, ac], axis=-1)
        return jnp.cos(ang), jnp.sin(ang)

    cr, sr = half(RET_DK // 4)
    cosr, sinr = jnp.tile(cr, (1, 2)), jnp.tile(sr, (1, 2))
    cm, sm = half(MLA_ROPE // 4)
    pad = lambda a, fill: jnp.concatenate(
        [jnp.full((T, MLA_NOPE), fill, F32), a, jnp.full((T, LANES - MLA_NOPE - MLA_ROPE), fill, F32)], axis=-1)
    return cosr, sinr, pad(cm, 1.0), pad(sm, 0.0)


def _layer_weights(l, w_in, mla_w_uq, mla_w_ukv, w_out, ffn_w_in, ffn_w_out):
    wi = w_in[l]
    offs = [0]
    for s in (2 * CONV_CH, 256, 256, 256, 256, MLA_Q_RANK, MLA_KV_RANK, MLA_ROPE):
        offs.append(offs[-1] + s)
    wa, wq, wk, wv, wg, wcq, wckv, wkr = [wi[:, offs[i]:offs[i + 1]] for i in range(8)]
    wk = wk * (RET_DK ** -0.5)
    D = wi.shape[0]
    zl = jnp.zeros((D, MLA_NOPE), F32)
    zr = jnp.zeros((D, LANES - MLA_NOPE - MLA_ROPE), F32)
    w_ext = jnp.concatenate(
        [wa, wq, _rot_cols(wq, RET_DK // 4), wk, _rot_cols(wk, RET_DK // 4), wv, wg, wcq, wckv,
         zl, wkr, zr, zl, _rot_cols(wkr, MLA_ROPE // 4), zr], axis=1).astype(BF16)

    uq = mla_w_uq[l].reshape(MLA_Q_RANK, MLA_HEADS, MLA_NOPE + MLA_ROPE)
    uq_rope = uq[:, :, MLA_NOPE:]
    zq = jnp.zeros((MLA_Q_RANK, MLA_HEADS, LANES - MLA_NOPE - MLA_ROPE), F32)
    uq_pad = jnp.concatenate([uq, zq], axis=-1).reshape(MLA_Q_RANK, MQ_W)
    uq_rot = jnp.concatenate([jnp.zeros_like(uq[:, :, :MLA_NOPE]), _rot_cols(uq_rope, MLA_ROPE // 4), zq],
                             axis=-1).reshape(MLA_Q_RANK, MQ_W)
    wuq = jnp.concatenate([uq_pad, uq_rot], axis=1).astype(BF16)

    ukv = mla_w_ukv[l].reshape(MLA_KV_RANK, MLA_HEADS, MLA_NOPE + MLA_V)
    uk_pad = jnp.concatenate([ukv[:, :, :MLA_NOPE], jnp.zeros((MLA_KV_RANK, MLA_HEADS, LANES - MLA_NOPE), F32)],
                             axis=-1).reshape(MLA_KV_RANK, MQ_W)
    uv = ukv[:, :, MLA_NOPE:].reshape(MLA_KV_RANK, MLA_HEADS * MLA_V)
    wukv = jnp.concatenate([uk_pad, uv], axis=1).astype(BF16)
    return w_ext, wuq, wukv, w_out[l].astype(BF16), ffn_w_in[l].astype(BF16), ffn_w_out[l].astype(BF16)


def kernel(x, c, ctx, c_ctx, mod_w, mod_b, pre1_g, post1_g, pre2_g, post2_g, w_in, conv_w, conv_b,
           conv_ln_g, conv_ln_b, ret_log_decay, ret_gn_g, mla_q_norm_g, mla_w_uq, mla_kv_norm_g,
           mla_w_ukv, w_out, ffn_w_in, ffn_w_out):
    B, T, D = x.shape
    TC = ctx.shape[1]
    cv = jnp.concatenate([c, c_ctx[None], jnp.zeros((8 - B - 1, D), F32)], axis=0)
    mods = _modulation(cv, mod_w, mod_b).reshape(DEPTH, 8, 6, 1, D)

    cosr, sinr, cosm, sinm = _rope_tables(T)
    ones = jnp.ones((TC, LANES), F32)
    zeros = jnp.zeros((TC, LANES), F32)
    zstate = jnp.zeros((B, RET_HEADS // 2, LANES, LANES), F32)

    xc = ctx
    for l in range(DEPTH):
        last = l == DEPTH - 1
        w_ext, wuq, wukv, wo, w1, w2 = _layer_weights(l, w_in, mla_w_uq, mla_w_ukv, w_out, ffn_w_in, ffn_w_out)
        lat = [mods[l, :B, j] for j in range(6)]
        cx = [jnp.broadcast_to(mods[l, B, j][None], (B, 1, D)) for j in range(6)]
        vec = lambda a: a[l][None]
        lgd = ret_log_decay[l]
        lgf = jnp.repeat(lgd[0], RET_DK)[None]
        lgb = jnp.repeat(lgd[1], RET_DK)[None]
        cw = jnp.concatenate([conv_w[l], jnp.zeros((1, CONV_CH), F32)], axis=0)

        proj = lambda xx, md, tabs: _inproj(xx, md[0], md[1], vec(pre1_g), w_ext, vec(mla_q_norm_g), wuq,
                                            vec(mla_kv_norm_g), wukv, *tabs)
        aL, rqL, rkL, rvL, rgL, mqL, mkL, mveL, mvoL = proj(x, lat, (cosr, sinr, cosm, sinm))
        aC, rqC, rkC, rvC, rgC, mqC, mkC, mveC, mvoC = proj(xc, cx, (ones, zeros, ones, zeros))

        yrC, sf, sb = _retention(lgd, rqC, rkC, rvC, rgC, vec(ret_gn_g), lgf, lgb, zstate, zstate)
        yrL, _, _ = _retention(lgd, rqL, rkL, rvL, rgL, vec(ret_gn_g), lgf, lgb, sf, sb)

        attL = _attention(mqL, [(mkC, mveC, mvoC), (mkL, mveL, mvoL)])
        mix = lambda a, yr, att, md, xx: _mixer(a, cw, vec(conv_b), vec(conv_ln_g), vec(conv_ln_b), yr, att, wo,
                                                vec(post1_g), md[2], xx)
        ffn = lambda xx, md: _ffn(xx, md[3], md[4], vec(pre2_g), w1, w2, vec(post2_g), md[5])
        x = mix(aL, yrL, attL, lat, x)
        if not last:
            attC = _attention(mqC, [(mkC, mveC, mvoC)])
            xc = mix(aC, yrC, attC, cx, xc)
        x = ffn(x, lat)
        if not last:
            xc = ffn(xc, cx)
    return x
```

```python
import functools

import jax
import jax.numpy as jnp
from jax import lax
from jax.experimental import pallas as pl
from jax.experimental.pallas import tpu as pltpu

F32 = jnp.float32
BF16 = jnp.bfloat16

D_MODEL = 1024
DEPTH = 2
GRID_W = 64
EPS = 1e-6
LOG2E = 1.4426950408889634
ROPE_BASE = 10000.0
CONV_CH = 256
CONV_K = 31
RET_HEADS = 4
RET_DK = 64
RET_DV = 64
MLA_HEADS = 8
MLA_Q_RANK = 256
MLA_KV_RANK = 128
MLA_NOPE = 64
MLA_ROPE = 32
MLA_V = 64
D_FF = 2816

LANES = 128
HALO = 16
ROW_TILE = 256
RET_CHUNK = 256
ATT_TQ = 512
ATT_TK = 512
ATT_AHEAD = 2
ATT_SLOTS = ATT_AHEAD + 1
ATT_ROWS = 32
FF_CHUNK = 1408
VMEM_LIMIT = 56 * 1024 * 1024

_OFF_A = 0
_OFF_Q = 512
_OFF_QR = 768
_OFF_K = 1024
_OFF_KR = 1280
_OFF_V = 1536
_OFF_G = 1792
_OFF_CQ = 2048
_OFF_CKV = 2304
_OFF_KRP = 2432
_OFF_KRPR = 2560
D_EXT = 2688
MQ_W = MLA_HEADS * LANES


def _sigmoid(x):
    return 1.0 / (1.0 + jnp.exp(-x))


def _silu(x):
    return x * _sigmoid(x)


def _params(*sem):
    return pltpu.CompilerParams(dimension_semantics=sem, vmem_limit_bytes=VMEM_LIMIT)


def _resident(shape):
    n = len(shape)
    return pl.BlockSpec(shape, lambda *_: (0,) * n, pipeline_mode=pl.Buffered(1))


def _mod_kernel(cv_ref, w_ref, b_ref, o_ref):
    s = _silu(cv_ref[...])
    o_ref[0] = jnp.dot(s.astype(BF16), w_ref[0].astype(BF16), preferred_element_type=F32) + b_ref[0]


def _modulation(cv, mod_w, mod_b):
    L, D, N = mod_w.shape
    tn = 1536
    return pl.pallas_call(
        _mod_kernel,
        out_shape=jax.ShapeDtypeStruct((L, 8, N), F32),
        grid=(L, N // tn),
        in_specs=[pl.BlockSpec((8, D), lambda l, j: (0, 0)),
                  pl.BlockSpec((1, D, tn), lambda l, j: (l, 0, j)),
                  pl.BlockSpec((1, 1, tn), lambda l, j: (l, 0, j))],
        out_specs=pl.BlockSpec((1, 8, tn), lambda l, j: (l, 0, j)),
        compiler_params=_params("arbitrary", "arbitrary"),
        name="modulation",
    )(cv, mod_w, mod_b.reshape(L, 1, N))


def _inproj_kernel(x_ref, sh_ref, sc_ref, g_ref, w_ref, qng_ref, wuq_ref, kvg_ref, wukv_ref,
                   cosr_ref, sinr_ref, cosm_ref, sinm_ref,
                   a_ref, rq_ref, rk_ref, rv_ref, rg_ref, mq_ref, mk_ref, mve_ref, mvo_ref):
    x = x_ref[0]
    h = x * lax.rsqrt(jnp.mean(x * x, axis=-1, keepdims=True) + EPS) * g_ref[...]
    h = h * (1.0 + sc_ref[0]) + sh_ref[0]
    p = jnp.dot(h.astype(BF16), w_ref[...], preferred_element_type=F32)

    a_ref[0] = p[:, _OFF_A:_OFF_A + CONV_CH] * _sigmoid(p[:, _OFF_A + CONV_CH:_OFF_A + 2 * CONV_CH])

    cosr = cosr_ref[...]
    sinr = sinr_ref[...]
    for i in range(2):
        lo, hi = i * LANES, (i + 1) * LANES
        rq_ref[0, :, lo:hi] = (p[:, _OFF_Q + lo:_OFF_Q + hi] * cosr
                               + p[:, _OFF_QR + lo:_OFF_QR + hi] * sinr).astype(BF16)
        rk_ref[0, :, lo:hi] = (p[:, _OFF_K + lo:_OFF_K + hi] * cosr
                               + p[:, _OFF_KR + lo:_OFF_KR + hi] * sinr).astype(BF16)
    rv_ref[0] = p[:, _OFF_V:_OFF_V + 256].astype(BF16)
    rg_ref[0] = p[:, _OFF_G:_OFF_G + 256]

    cosm = cosm_ref[...]
    sinm = sinm_ref[...]
    cq = p[:, _OFF_CQ:_OFF_CQ + MLA_Q_RANK]
    qn = cq * lax.rsqrt(jnp.mean(cq * cq, axis=-1, keepdims=True) + EPS) * qng_ref[...]
    qq = jnp.dot(qn.astype(BF16), wuq_ref[...], preferred_element_type=F32)
    scale = float((MLA_NOPE + MLA_ROPE) ** -0.5 * LOG2E)
    ckv = p[:, _OFF_CKV:_OFF_CKV + MLA_KV_RANK]
    kvn = ckv * lax.rsqrt(jnp.mean(ckv * ckv, axis=-1, keepdims=True) + EPS) * kvg_ref[...]
    kv = jnp.dot(kvn.astype(BF16), wukv_ref[...], preferred_element_type=F32)
    kr_slot = p[:, _OFF_KRP:_OFF_KRP + LANES] * cosm + p[:, _OFF_KRPR:_OFF_KRPR + LANES] * sinm
    for hh in range(MLA_HEADS):
        lo, hi = hh * LANES, (hh + 1) * LANES
        mq_ref[0, :, lo:hi] = ((qq[:, lo:hi] * cosm + qq[:, MQ_W + lo:MQ_W + hi] * sinm) * scale).astype(BF16)
        mk_ref[0, :, lo:hi] = (kv[:, lo:hi] + kr_slot).astype(BF16)
    v = kv[:, MQ_W:]
    lane = lax.broadcasted_iota(jnp.int32, v.shape, 1)
    lane = lane % LANES
    even = lane < MLA_V
    mve_ref[0] = jnp.where(even, v, jnp.where(lane == MLA_V, 1.0, 0.0)).astype(BF16)
    mvo_ref[0] = jnp.where(even, jnp.where(lane == 0, 1.0, 0.0), v).astype(BF16)


def _inproj(x, sh, sc, g, w_ext, qng, wuq, kvg, wukv, cosr, sinr, cosm, sinm):
    B, L, D = x.shape
    tm = ROW_TILE
    row = lambda w: pl.BlockSpec((1, tm, w), lambda b, i: (b, i, 0))
    vec = lambda w: pl.BlockSpec((1, 1, w), lambda b, i: (b, 0, 0))
    tab = pl.BlockSpec((tm, LANES), lambda b, i: (i, 0))
    outs = [(CONV_CH, F32), (256, BF16), (256, BF16), (256, BF16), (256, F32),
            (MQ_W, BF16), (MQ_W, BF16), (512, BF16), (512, BF16)]
    return pl.pallas_call(
        _inproj_kernel,
        out_shape=[jax.ShapeDtypeStruct((B, L, w), dt) for w, dt in outs],
        grid=(B, L // tm),
        in_specs=[row(D), vec(D), vec(D), _resident((1, D)), _resident(w_ext.shape),
                  _resident((1, MLA_Q_RANK)), _resident(wuq.shape),
                  _resident((1, MLA_KV_RANK)), _resident(wukv.shape), tab, tab, tab, tab],
        out_specs=[row(w) for w, _ in outs],
        compiler_params=_params("arbitrary", "arbitrary"),
        name="inproj",
    )(x, sh, sc, g, w_ext, qng, wuq, kvg, wukv, cosr, sinr, cosm, sinm)


def _ret_kernel(lgd_ref, q_ref, k_ref, v_ref, g_ref, gng_ref, lgf_ref, lgb_ref, sf0_ref, sb0_ref,
                y_ref, sfo_ref, sbo_ref, w_scr, kvf_scr, kvb_scr, sfs_scr, sbs_scr, *, n_chunks):
    C = RET_CHUNK
    N = n_chunks

    @pl.when(pl.program_id(0) == 0)
    def _():
        t = lax.broadcasted_iota(jnp.int32, (C, C), 0)
        m = lax.broadcasted_iota(jnp.int32, (C, C), 1)
        d = (t - m).astype(F32)
        for hh in range(RET_HEADS):
            w_scr[hh] = jnp.exp(jnp.where(d >= 0.0, lgd_ref[0, hh] * d, lgd_ref[1, hh] * (-d)))

    j = lax.broadcasted_iota(jnp.int32, (C, LANES), 0).astype(F32)
    lane = lax.broadcasted_iota(jnp.int32, (C, LANES), 1)
    first = lane < RET_DV
    r128 = lax.broadcasted_iota(jnp.int32, (LANES, LANES), 0)
    c128 = lax.broadcasted_iota(jnp.int32, (LANES, LANES), 1)
    blockdiag = (r128 < RET_DK) == (c128 < RET_DV)

    for pi in range(RET_HEADS // 2):
        lo, hi = pi * LANES, (pi + 1) * LANES
        lgf = lgf_ref[:, lo:hi]
        lgb = lgb_ref[:, lo:hi]
        zeta_f = jnp.exp(lgf * (C - 1.0 - j))
        zeta_b = jnp.exp(lgb * j)
        xi_f = jnp.exp(lgf * (j + 1.0))
        xi_b = jnp.exp(lgb * (C - j))
        gc_f = jnp.exp(lgf * float(C))
        gc_b = jnp.exp(lgb * float(C))

        def chunk_sums(i, carry):
            rows = pl.ds(pl.multiple_of(i * C, C), C)
            kf = k_ref[0, rows, lo:hi].astype(F32)
            vv = v_ref[0, rows, lo:hi]
            kzf = (kf * zeta_f).T.astype(BF16)
            kzb = (kf * zeta_b).T.astype(BF16)
            kvf_scr[i] = jnp.where(blockdiag, jnp.dot(kzf, vv, preferred_element_type=F32), 0.0)
            kvb_scr[i] = jnp.where(blockdiag, jnp.dot(kzb, vv, preferred_element_type=F32), 0.0)
            return carry

        lax.fori_loop(0, N, chunk_sums, 0)

        def fwd_scan(i, s):
            sfs_scr[i] = s
            return s * gc_f + kvf_scr[i]

        sfo_ref[0, pi] = lax.fori_loop(0, N, fwd_scan, sf0_ref[0, pi])

        def bwd_scan(ii, s):
            i = N - 1 - ii
            sbs_scr[i] = s
            return s * gc_b + kvb_scr[i]

        sbo_ref[0, pi] = lax.fori_loop(0, N, bwd_scan, sb0_ref[0, pi])

        gng = gng_ref[:, lo:hi]

        def chunk_out(i, carry):
            rows = pl.ds(pl.multiple_of(i * C, C), C)
            q = q_ref[0, rows, lo:hi]
            k = k_ref[0, rows, lo:hi]
            vv = v_ref[0, rows, lo:hi]
            zero = jnp.zeros_like(q)
            dn = (((1,), (1,)), ((), ()))
            s0 = lax.dot_general(jnp.where(first, q, zero), k, dn, preferred_element_type=F32)
            s1 = lax.dot_general(jnp.where(first, zero, q), k, dn, preferred_element_type=F32)
            p0 = (s0 * w_scr[2 * pi]).astype(BF16)
            p1 = (s1 * w_scr[2 * pi + 1]).astype(BF16)
            o = jnp.dot(p0, jnp.where(first, vv, zero), preferred_element_type=F32)
            o = o + jnp.dot(p1, jnp.where(first, zero, vv), preferred_element_type=F32)
            o = o + jnp.dot(q, sfs_scr[i].astype(BF16), preferred_element_type=F32) * xi_f
            o = o + jnp.dot(q, sbs_scr[i].astype(BF16), preferred_element_type=F32) * xi_b
            inv = 1.0 / RET_DV
            tot = jnp.sum(o, axis=-1, keepdims=True)
            m0 = jnp.sum(jnp.where(first, o, 0.0), axis=-1, keepdims=True)
            mu = jnp.where(first, m0, tot - m0) * inv
            dlt = o - mu
            d2 = dlt * dlt
            tot2 = jnp.sum(d2, axis=-1, keepdims=True)
            v0 = jnp.sum(jnp.where(first, d2, 0.0), axis=-1, keepdims=True)
            var = jnp.where(first, v0, tot2 - v0) * inv
            y = dlt * lax.rsqrt(var + EPS) * gng
            y_ref[0, rows, lo:hi] = (y * _silu(g_ref[0, rows, lo:hi])).astype(BF16)
            return carry

        lax.fori_loop(0, N, chunk_out, 0)


def _retention(lgd, q, k, v, g, gng, lgf, lgb, sf0, sb0):
    B, L, W = q.shape
    C = RET_CHUNK
    N = L // C
    P = RET_HEADS // 2
    seq = lambda: pl.BlockSpec((1, L, W), lambda b: (b, 0, 0))
    st = lambda: pl.BlockSpec((1, P, LANES, LANES), lambda b: (b, 0, 0, 0))
    st_shape = jax.ShapeDtypeStruct((B, P, LANES, LANES), F32)
    return pl.pallas_call(
        functools.partial(_ret_kernel, n_chunks=N),
        out_shape=[jax.ShapeDtypeStruct((B, L, W), BF16), st_shape, st_shape],
        grid=(B,),
        in_specs=[pl.BlockSpec(memory_space=pltpu.SMEM), seq(), seq(), seq(), seq(),
                  pl.BlockSpec((1, W), lambda b: (0, 0)), pl.BlockSpec((1, W), lambda b: (0, 0)),
                  pl.BlockSpec((1, W), lambda b: (0, 0)), st(), st()],
        out_specs=[seq(), st(), st()],
        scratch_shapes=[pltpu.VMEM((RET_HEADS, C, C), F32),
                        pltpu.VMEM((N, LANES, LANES), F32), pltpu.VMEM((N, LANES, LANES), F32),
                        pltpu.VMEM((N, LANES, LANES), F32), pltpu.VMEM((N, LANES, LANES), F32)],
        compiler_params=_params("arbitrary"),
        name="retention",
    )(lgd, q, k, v, g, gng, lgf, lgb, sf0, sb0)


def _attn_kernel(*refs, lengths):
    q_ref = refs[0]
    src = refs[1:1 + 3 * len(lengths)]
    o_ref, s_scr, p_scr, m_scr, a_scr, acc_scr = refs[1 + 3 * len(lengths):]
    tq = q_ref.shape[1]
    dn = (((1,), (1,)), ((), ()))
    H = MLA_HEADS
    RB = ATT_ROWS

    m_scr[...] = jnp.full(m_scr.shape, -1e30, F32)
    acc_scr[...] = jnp.zeros(acc_scr.shape, F32)

    def step(k_ref, ve_ref, vo_ref, rows, tk):
        def scores(h):
            q = q_ref[0, :, h * LANES:(h + 1) * LANES]
            k = k_ref[0, rows, h * LANES:(h + 1) * LANES]
            s_scr[h % ATT_SLOTS, :, 0:tk] = lax.dot_general(q, k, dn, preferred_element_type=F32)

        for h in range(ATT_AHEAD):
            scores(h)
        for h in range(H):
            if h + ATT_AHEAD < H:
                scores(h + ATT_AHEAD)
            for r0 in range(0, tq, RB):
                s = s_scr[h % ATT_SLOTS, r0:r0 + RB, 0:tk]
                m_old = m_scr[h, r0:r0 + RB, :]
                n = jnp.maximum(m_old, jnp.max(s, axis=-1, keepdims=True))
                p_scr[h % 2, r0:r0 + RB, 0:tk] = jnp.exp2(s - jnp.tile(n, (1, tk // LANES))).astype(BF16)
                a_scr[h % 2, r0:r0 + RB, :] = jnp.exp2(m_old - n)
                m_scr[h, r0:r0 + RB, :] = n
            v_ref = ve_ref if h % 2 == 0 else vo_ref
            v = v_ref[0, rows, (h // 2) * LANES:(h // 2 + 1) * LANES]
            acc_scr[h] = acc_scr[h] * a_scr[h % 2] + jnp.dot(p_scr[h % 2, :, 0:tk], v, preferred_element_type=F32)

    for si, length in enumerate(lengths):
        k_ref, ve_ref, vo_ref = src[3 * si:3 * si + 3]
        tk = min(ATT_TK, length)
        n = length // tk
        if n == 1:
            step(k_ref, ve_ref, vo_ref, pl.ds(0, tk), tk)
        else:
            @pl.loop(0, n)
            def _(c):
                step(k_ref, ve_ref, vo_ref, pl.ds(pl.multiple_of(c * tk, tk), tk), tk)

    first = lax.broadcasted_iota(jnp.int32, (tq, LANES), 1) < MLA_V
    for pj in range(H // 2):
        a0, a1 = acc_scr[2 * pj], acc_scr[2 * pj + 1]
        out = jnp.where(first, a0 / a0[:, MLA_V:MLA_V + 1], a1 / a1[:, 0:1])
        o_ref[0, :, pj * LANES:(pj + 1) * LANES] = out.astype(BF16)


def _attention(q, sources):
    B, L, _ = q.shape
    tq = min(ATT_TQ, L)
    lengths =tuple(s[0].shape[1] for s in sources)
    in_specs = [pl.BlockSpec((1, tq, MQ_W), lambda b, i: (b, i, 0))]
    args = [q]
    for k, ve, vo in sources:
        S = k.shape[1]
        in_specs += [pl.BlockSpec((1, S, MQ_W), lambda b, i: (b, 0, 0)),
                     pl.BlockSpec((1, S, 512), lambda b, i: (b, 0, 0)),
                     pl.BlockSpec((1, S, 512), lambda b, i: (b, 0, 0))]
        args += [k, ve, vo]
    return pl.pallas_call(
        functools.partial(_attn_kernel, lengths=lengths),
        out_shape=jax.ShapeDtypeStruct((B, L, MLA_HEADS * MLA_V), BF16),
        grid=(B, L // tq),
        in_specs=in_specs,
        out_specs=pl.BlockSpec((1, tq, MLA_HEADS * MLA_V), lambda b, i: (b, i, 0)),
        scratch_shapes=[pltpu.VMEM((ATT_SLOTS, tq, ATT_TK), F32),
                        pltpu.VMEM((2, tq, ATT_TK), BF16),
                        pltpu.VMEM((MLA_HEADS, tq, LANES), F32),
                        pltpu.VMEM((2, tq, LANES), F32),
                        pltpu.VMEM((MLA_HEADS, tq, LANES), F32)],
        compiler_params=_params("arbitrary", "arbitrary"),
        name="attention",
    )(*args)


def _mixer_kernel(a_ref, ap_ref, an_ref, cw_ref, cb_ref, lg_ref, lb_ref, yr_ref, at_ref, wo_ref,
                  pg_ref, gate_ref, x_ref, o_ref, win_scr, yc_scr):
    tm = a_ref.shape[1]
    i = pl.program_id(1)
    n = pl.num_programs(1)
    win_scr[0:HALO] = jnp.where(i > 0, ap_ref[0], 0.0)
    win_scr[HALO:HALO + tm] = a_ref[0]
    win_scr[HALO + tm:HALO + tm + HALO] = jnp.where(i < n - 1, an_ref[0], 0.0)
    sub = 64
    off = HALO - CONV_K // 2
    for r0 in range(0, tm, sub):
        acc = jnp.zeros((sub, CONV_CH), F32) + cb_ref[...]
        for t in range(CONV_K):
            acc = acc + cw_ref[t:t + 1, :] * win_scr[r0 + off + t:r0 + off + t + sub, :]
        mu = jnp.mean(acc, axis=-1, keepdims=True)
        d = acc - mu
        var = jnp.mean(d * d, axis=-1, keepdims=True)
        yc = d * lax.rsqrt(var + EPS) * lg_ref[...] + lb_ref[...]
        yc_scr[r0:r0 + sub, :] = _silu(yc).astype(BF16)
    y = jnp.dot(yc_scr[...], wo_ref[0:256, :], preferred_element_type=F32)
    y = y + jnp.dot(yr_ref[0], wo_ref[256:512, :], preferred_element_type=F32)
    y = y + jnp.dot(at_ref[0], wo_ref[512:1024, :], preferred_element_type=F32)
    r = y * lax.rsqrt(jnp.mean(y * y, axis=-1, keepdims=True) + EPS) * pg_ref[...]
    o_ref[0] = x_ref[0] + gate_ref[0] * r


def _mixer(a, cw, cb, lg, lb, yr, att, wo, pg, gate, x):
    B, L, D = x.shape
    tm = ROW_TILE
    hb = tm // HALO
    nh = L // HALO
    row = lambda w: pl.BlockSpec((1, tm, w), lambda b, i: (b, i, 0))
    return pl.pallas_call(
        _mixer_kernel,
        out_shape=jax.ShapeDtypeStruct((B, L, D), F32),
        grid=(B, L // tm),
        in_specs=[row(CONV_CH),
                  pl.BlockSpec((1, HALO, CONV_CH), lambda b, i: (b, jnp.maximum(i * hb - 1, 0), 0)),
                  pl.BlockSpec((1, HALO, CONV_CH), lambda b, i: (b, jnp.minimum((i + 1) * hb, nh - 1), 0)),
                  _resident(cw.shape), _resident((1, CONV_CH)), _resident((1, CONV_CH)),
                  _resident((1, CONV_CH)), row(256), row(512), _resident(wo.shape),
                  _resident((1, D)), pl.BlockSpec((1, 1, D), lambda b, i: (b, 0, 0)), row(D)],
        out_specs=row(D),
        scratch_shapes=[pltpu.VMEM((tm + 2 * HALO, CONV_CH), F32), pltpu.VMEM((tm, CONV_CH), BF16)],
        compiler_params=_params("arbitrary", "arbitrary"),
        name="mixer_out",
    )(a, a, a, cw, cb, lg, lb, yr, att, wo, pg, gate, x)


def _ffn_kernel(x_ref, sh_ref, sc_ref, g_ref, w1_ref, w2_ref, pg_ref, gate_ref, o_ref):
    x = x_ref[0]
    h = x * lax.rsqrt(jnp.mean(x * x, axis=-1, keepdims=True) + EPS) * g_ref[...]
    h = (h * (1.0 + sc_ref[0]) + sh_ref[0]).astype(BF16)
    y = jnp.zeros(x.shape, F32)
    for c in range(D_FF // FF_CHUNK):
        lo, hi = c * FF_CHUNK, (c + 1) * FF_CHUNK
        u = jnp.dot(h, w1_ref[:, lo:hi], preferred_element_type=F32)
        gt = jnp.dot(h, w1_ref[:, D_FF + lo:D_FF + hi], preferred_element_type=F32)
        y = y + jnp.dot((_silu(gt) * u).astype(BF16), w2_ref[lo:hi, :], preferred_element_type=F32)
    r = y * lax.rsqrt(jnp.mean(y * y, axis=-1, keepdims=True) + EPS) * pg_ref[...]
    o_ref[0] = x + gate_ref[0] * r


def _ffn(x, sh, sc, g, w1, w2, pg, gate):
    B, L, D = x.shape
    tm = ROW_TILE
    row = pl.BlockSpec((1, tm, D), lambda b, i: (b, i, 0))
    vec = pl.BlockSpec((1, 1, D), lambda b, i: (b, 0, 0))
    return pl.pallas_call(
        _ffn_kernel,
        out_shape=jax.ShapeDtypeStruct((B, L, D), F32),
        grid=(B, L // tm),
        in_specs=[row, vec, vec, _resident((1, D)), _resident(w1.shape), _resident(w2.shape),
                  _resident((1, D)), vec],
        out_specs=row,
        compiler_params=_params("arbitrary", "arbitrary"),
        name="ffn",
    )(x, sh, sc, g, w1, w2, pg, gate)


def _rot_cols(w, f):
    lead = w.shape[:-1]
    wr = w.reshape(lead + (-1, 2, f))
    return jnp.stack([-wr[..., 1, :], wr[..., 0, :]], axis=-2).reshape(w.shape)


def _rope_tables(T):
    t = jnp.arange(T, dtype=jnp.int32)
    rowp = (t // GRID_W).astype(F32)[:, None]
    colp = (t % GRID_W).astype(F32)[:, None]

    def half(f):
        inv = ROPE_BASE ** (-jnp.arange(f, dtype=F32) / f)
        ar, ac = rowp * inv[None, :], colp * inv[None, :]
        ang = jnp.concatenate([ar, ar, ac, ac], axis=-1)
        return jnp.cos(ang), jnp.sin(ang)

    cr, sr = half(RET_DK // 4)
    cosr, sinr = jnp.tile(cr, (1, 2)), jnp.tile(sr, (1, 2))
    cm, sm = half(MLA_ROPE // 4)
    pad = lambda a, fill: jnp.concatenate(
        [jnp.full((T, MLA_NOPE), fill, F32), a, jnp.full((T, LANES - MLA_NOPE - MLA_ROPE), fill, F32)], axis=-1)
    return cosr, sinr, pad(cm, 1.0), pad(sm, 0.0)


def _layer_weights(l, w_in, mla_w_uq, mla_w_ukv, w_out, ffn_w_in, ffn_w_out):
    wi = w_in[l]
    offs = [0]
    for s in (2 * CONV_CH, 256, 256, 256, 256, MLA_Q_RANK, MLA_KV_RANK, MLA_ROPE):
        offs.append(offs[-1] + s)
    wa, wq, wk, wv, wg, wcq, wckv, wkr = [wi[:, offs[i]:offs[i + 1]] for i in range(8)]
    wk = wk * (RET_DK ** -0.5)
    D = wi.shape[0]
    zl = jnp.zeros((D, MLA_NOPE), F32)
    zr = jnp.zeros((D, LANES - MLA_NOPE - MLA_ROPE), F32)
    w_ext = jnp.concatenate(
        [wa, wq, _rot_cols(wq, RET_DK // 4), wk, _rot_cols(wk, RET_DK // 4), wv, wg, wcq, wckv,
         zl, wkr, zr, zl, _rot_cols(wkr, MLA_ROPE // 4), zr], axis=1).astype(BF16)

    uq = mla_w_uq[l].reshape(MLA_Q_RANK, MLA_HEADS, MLA_NOPE + MLA_ROPE)
    uq_rope = uq[:, :, MLA_NOPE:]
    zq = jnp.zeros((MLA_Q_RANK, MLA_HEADS, LANES - MLA_NOPE - MLA_ROPE), F32)
    uq_pad = jnp.concatenate([uq, zq], axis=-1).reshape(MLA_Q_RANK, MQ_W)
    uq_rot = jnp.concatenate([jnp.zeros_like(uq[:, :, :MLA_NOPE]), _rot_cols(uq_rope, MLA_ROPE // 4), zq],
                             axis=-1).reshape(MLA_Q_RANK, MQ_W)
    wuq = jnp.concatenate([uq_pad, uq_rot], axis=1).astype(BF16)

    ukv = mla_w_ukv[l].reshape(MLA_KV_RANK, MLA_HEADS, MLA_NOPE + MLA_V)
    uk_pad = jnp.concatenate([ukv[:, :, :MLA_NOPE], jnp.zeros((MLA_KV_RANK, MLA_HEADS, LANES - MLA_NOPE), F32)],
                             axis=-1).reshape(MLA_KV_RANK, MQ_W)
    uv = ukv[:, :, MLA_NOPE:].reshape(MLA_KV_RANK, MLA_HEADS * MLA_V)
    wukv = jnp.concatenate([uk_pad, uv], axis=1).astype(BF16)
    return w_ext, wuq, wukv, w_out[l].astype(BF16), ffn_w_in[l].astype(BF16), ffn_w_out[l].astype(BF16)


def kernel(x, c, ctx, c_ctx, mod_w, mod_b, pre1_g, post1_g, pre2_g, post2_g, w_in, conv_w, conv_b,
           conv_ln_g, conv_ln_b, ret_log_decay, ret_gn_g, mla_q_norm_g, mla_w_uq, mla_kv_norm_g,
           mla_w_ukv, w_out, ffn_w_in, ffn_w_out):
    B, T, D = x.shape
    TC = ctx.shape[1]
    cv = jnp.concatenate([c, c_ctx[None], jnp.zeros((8 - B - 1, D), F32)], axis=0)
    mods = _modulation(cv, mod_w, mod_b).reshape(DEPTH, 8, 6, 1, D)

    cosr, sinr, cosm, sinm = _rope_tables(T)
    ones = jnp.ones((TC, LANES), F32)
    zeros = jnp.zeros((TC, LANES), F32)
    zstate = jnp.zeros((B, RET_HEADS // 2, LANES, LANES), F32)

    xc = ctx
    for l in range(DEPTH):
        last = l == DEPTH - 1
        w_ext, wuq, wukv, wo, w1, w2 = _layer_weights(l, w_in, mla_w_uq, mla_w_ukv, w_out, ffn_w_in, ffn_w_out)
        lat = [mods[l, :B, j] for j in range(6)]
        cx = [jnp.broadcast_to(mods[l, B, j][None], (B, 1, D)) for j in range(6)]
        vec = lambda a: a[l][None]
        lgd = ret_log_decay[l]
        lgf = jnp.repeat(lgd[0], RET_DK)[None]
        lgb = jnp.repeat(lgd[1], RET_DK)[None]
        cw = jnp.concatenate([conv_w[l], jnp.zeros((1, CONV_CH), F32)], axis=0)

        proj = lambda xx, md, tabs: _inproj(xx, md[0], md[1], vec(pre1_g), w_ext, vec(mla_q_norm_g), wuq,
                                            vec(mla_kv_norm_g), wukv, *tabs)
        aL, rqL, rkL, rvL, rgL, mqL, mkL, mveL, mvoL = proj(x, lat, (cosr, sinr, cosm, sinm))
        aC, rqC, rkC, rvC, rgC, mqC, mkC, mveC, mvoC = proj(xc, cx, (ones, zeros, ones, zeros))

        yrC, sf, sb = _retention(lgd, rqC, rkC, rvC, rgC, vec(ret_gn_g), lgf, lgb, zstate, zstate)
        yrL, _, _ = _retention(lgd, rqL, rkL, rvL, rgL, vec(ret_gn_g), lgf, lgb, sf, sb)

        attL = _attention(mqL, [(mkC, mveC, mvoC), (mkL, mveL, mvoL)])
        mix = lambda a, yr, att, md, xx: _mixer(a, cw, vec(conv_b), vec(conv_ln_g), vec(conv_ln_b), yr, att, wo,
                                                vec(post1_g), md[2], xx)
        ffn = lambda xx, md: _ffn(xx, md[3], md[4], vec(pre2_g), w1, w2, vec(post2_g), md[5])
        x = mix(aL, yrL, attL, lat, x)
        if not last:
            attC = _attention(mqC, [(mkC, mveC, mvoC)])
            xc = mix(aC, yrC, attC, cx, xc)
        x = ffn(x, lat)
        if not last:
            xc = ffn(xc, cx)
    return x
```

```python
import functools

import jax
import jax.numpy as jnp
from jax import lax
from jax.experimental import pallas as pl
from jax.experimental.pallas import tpu as pltpu

F32 = jnp.float32
BF16 = jnp.bfloat16

D_MODEL = 1024
DEPTH = 2
GRID_W = 64
EPS = 1e-6
LOG2E = 1.4426950408889634
ROPE_BASE = 10000.0
CONV_CH = 256
CONV_K = 31
RET_HEADS = 4
RET_DK = 64
RET_DV = 64
MLA_HEADS = 8
MLA_Q_RANK = 256
MLA_KV_RANK = 128
MLA_NOPE = 64
MLA_ROPE = 32
MLA_V = 64
D_FF = 2816

LANES = 128
SUBLANES = 8
HALO = 16
CONV_SPAN = 24
ROW_TILE = 256
FFN_TILE = 512
RET_CHUNK = 256
ATT_TQ = 512
ATT_TK = 512
ATT_AHEAD = 2
ATT_SLOTS = ATT_AHEAD + 1
ATT_ROWS = 32
MXU_DIM = 256
FF_CHUNKS = ((0, 6 * MXU_DIM), (6 * MXU_DIM, D_FF))
VMEM_LIMIT = 56 * 1024 * 1024

_OFF_A = 0
_OFF_Q = 512
_OFF_QR = 768
_OFF_K = 1024
_OFF_KR = 1280
_OFF_V = 1536
_OFF_G = 1792
_OFF_CQ = 2048
_OFF_CKV = 2304
_OFF_KRP = 2432
_OFF_KRPR = 2560
D_EXT = 2688
MQ_W = MLA_HEADS * LANES


def _sigmoid(x):
    return 1.0 / (1.0 + jnp.exp(-x))


def _silu(x):
    return x * _sigmoid(x)


def _params(*sem):
    return pltpu.CompilerParams(dimension_semantics=sem, vmem_limit_bytes=VMEM_LIMIT)


def _resident(shape):
    n = len(shape)
    return pl.BlockSpec(shape, lambda *_: (0,) * n, pipeline_mode=pl.Buffered(1))


def _mod_kernel(cv_ref, w_ref, b_ref, o_ref):
    s = _silu(cv_ref[...])
    o_ref[0] = jnp.dot(s.astype(BF16), w_ref[0].astype(BF16), preferred_element_type=F32) + b_ref[0]


def _modulation(cv, mod_w, mod_b):
    L, D, N = mod_w.shape
    tn = 1536
    return pl.pallas_call(
        _mod_kernel,
        out_shape=jax.ShapeDtypeStruct((L, 8, N), F32),
        grid=(L, N // tn),
        in_specs=[pl.BlockSpec((8, D), lambda l, j: (0, 0)),
                  pl.BlockSpec((1, D, tn), lambda l, j: (l, 0, j)),
                  pl.BlockSpec((1, 1, tn), lambda l, j: (l, 0, j))],
        out_specs=pl.BlockSpec((1, 8, tn), lambda l, j: (l, 0, j)),
        compiler_params=_params("arbitrary", "arbitrary"),
        name="modulation",
    )(cv, mod_w, mod_b.reshape(L, 1, N))


def _inproj_kernel(x_ref, sh_ref, sc_ref, g_ref, w_ref, qng_ref, wuq_ref, kvg_ref, wukv_ref,
                   cosr_ref, sinr_ref, cosm_ref, sinm_ref,
                   a_ref, rq_ref, rk_ref, rv_ref, rg_ref, mq_ref, mk_ref, mve_ref, mvo_ref):
    x = x_ref[0]
    h = x * lax.rsqrt(jnp.mean(x * x, axis=-1, keepdims=True) + EPS) * g_ref[...]
    h = h * (1.0 + sc_ref[0]) + sh_ref[0]
    p = jnp.dot(h.astype(BF16), w_ref[...], preferred_element_type=F32)

    a_ref[0] = p[:, _OFF_A:_OFF_A + CONV_CH] * _sigmoid(p[:, _OFF_A + CONV_CH:_OFF_A + 2 * CONV_CH])

    cosr = cosr_ref[...]
    sinr = sinr_ref[...]
    for i in range(2):
        lo, hi = i * LANES, (i + 1) * LANES
        rq_ref[0, :, lo:hi] = (p[:, _OFF_Q + lo:_OFF_Q + hi] * cosr
                               + p[:, _OFF_QR + lo:_OFF_QR + hi] * sinr).astype(BF16)
        rk_ref[0, :, lo:hi] = (p[:, _OFF_K + lo:_OFF_K + hi] * cosr
                               + p[:, _OFF_KR + lo:_OFF_KR + hi] * sinr).astype(BF16)
    rv_ref[0] = p[:, _OFF_V:_OFF_V + 256].astype(BF16)
    rg_ref[0] = p[:, _OFF_G:_OFF_G + 256]

    cosm = cosm_ref[...]
    sinm = sinm_ref[...]
    cq = p[:, _OFF_CQ:_OFF_CQ + MLA_Q_RANK]
    qn = cq * lax.rsqrt(jnp.mean(cq * cq, axis=-1, keepdims=True) + EPS) * qng_ref[...]
    qq = jnp.dot(qn.astype(BF16), wuq_ref[...], preferred_element_type=F32)
    scale = float((MLA_NOPE + MLA_ROPE) ** -0.5 * LOG2E)
    ckv = p[:, _OFF_CKV:_OFF_CKV + MLA_KV_RANK]
    kvn = ckv * lax.rsqrt(jnp.mean(ckv * ckv, axis=-1, keepdims=True) + EPS) * kvg_ref[...]
    kv = jnp.dot(kvn.astype(BF16), wukv_ref[...], preferred_element_type=F32)
    kr_slot = p[:, _OFF_KRP:_OFF_KRP + LANES] * cosm + p[:, _OFF_KRPR:_OFF_KRPR + LANES] * sinm
    for hh in range(MLA_HEADS):
        lo, hi = hh * LANES, (hh + 1) * LANES
        mq_ref[0, :, lo:hi] = ((qq[:, lo:hi] * cosm + qq[:, MQ_W + lo:MQ_W + hi] * sinm) * scale).astype(BF16)
        mk_ref[0, :, lo:hi] = (kv[:, lo:hi] + kr_slot).astype(BF16)
    v = kv[:, MQ_W:]
    lane = lax.broadcasted_iota(jnp.int32, v.shape, 1)
    lane = lane % LANES
    even = lane < MLA_V
    mve_ref[0] = jnp.where(even, v, jnp.where(lane == MLA_V, 1.0, 0.0)).astype(BF16)
    mvo_ref[0] = jnp.where(even, jnp.where(lane == 0, 1.0, 0.0), v).astype(BF16)


def _inproj(x, sh, sc, g, w_ext, qng, wuq, kvg, wukv, cosr, sinr, cosm, sinm):
    B, L, D = x.shape
    tm = ROW_TILE
    row = lambda w: pl.BlockSpec((1, tm, w), lambda b, i: (b, i, 0))
    vec = lambda w: pl.BlockSpec((1, 1, w), lambda b, i: (b, 0, 0))
    tab = pl.BlockSpec((tm, LANES), lambda b, i: (i, 0))
    outs = [(CONV_CH, F32), (256, BF16), (256, BF16), (256, BF16), (256, F32),
            (MQ_W, BF16), (MQ_W, BF16), (512, BF16), (512, BF16)]
    return pl.pallas_call(
        _inproj_kernel,
        out_shape=[jax.ShapeDtypeStruct((B, L, w), dt) for w, dt in outs],
        grid=(B, L // tm),
        in_specs=[row(D), vec(D), vec(D), _resident((1, D)), _resident(w_ext.shape),
                  _resident((1, MLA_Q_RANK)), _resident(wuq.shape),
                  _resident((1, MLA_KV_RANK)), _resident(wukv.shape), tab, tab, tab, tab],
        out_specs=[row(w) for w, _ in outs],
        compiler_params=_params("arbitrary", "arbitrary"),
        name="inproj",
    )(x, sh, sc, g, w_ext, qng, wuq, kvg, wukv, cosr, sinr, cosm, sinm)


def _ret_kernel(lgd_ref, q_ref, k_ref, v_ref, g_ref, gng_ref, lgf_ref, lgb_ref, sf0_ref, sb0_ref,
                y_ref, sfo_ref, sbo_ref, w_scr, dec_scr, kvf_scr, kvb_scr, sfs_scr, sbs_scr, *, n_chunks):
    C = RET_CHUNK
    N = n_chunks
    P = RET_HEADS // 2
    U = 2 if N % 2 == 0 else 1

    @pl.when(pl.program_id(0) == 0)
    def _():
        t = lax.broadcasted_iota(jnp.int32, (C, C), 0)
        m = lax.broadcasted_iota(jnp.int32, (C, C), 1)
        d = (t - m).astype(F32)
        for hh in range(RET_HEADS):
            w_scr[hh] = jnp.exp(jnp.where(d >= 0.0, lgd_ref[0, hh] * d, lgd_ref[1, hh] * (-d)))
        j = lax.broadcasted_iota(jnp.int32, (C, LANES), 0).astype(F32)
        for pi in range(P):
            lgf = lgf_ref[:, pi * LANES:(pi + 1) * LANES]
            lgb = lgb_ref[:, pi * LANES:(pi + 1) * LANES]
            dec_scr[pi, 0] = jnp.exp(lgf * (C - 1.0 - j))
            dec_scr[pi, 1] = jnp.exp(lgb * j)
            dec_scr[pi, 2] = jnp.exp(lgf * (j + 1.0))
            dec_scr[pi, 3] = jnp.exp(lgb * (C - j))

    first = lax.broadcasted_iota(jnp.int32, (C, LANES), 1) < RET_DV
    r128 = lax.broadcasted_iota(jnp.int32, (LANES, LANES), 0)
    c128 = lax.broadcasted_iota(jnp.int32, (LANES, LANES), 1)
    blockdiag = (r128 < RET_DK) == (c128 < RET_DV)
    dn = (((1,), (1,)), ((), ()))

    def sums_one(i, pi):
        rows = pl.ds(pl.multiple_of(i * C, C), C)
        lo, hi = pi * LANES, (pi + 1) * LANES
        kf = k_ref[0, rows, lo:hi].astype(F32)
        vv = v_ref[0, rows, lo:hi]
        kzf = (kf * dec_scr[pi, 0]).T.astype(BF16)
        kzb = (kf * dec_scr[pi, 1]).T.astype(BF16)
        kvf_scr[pi, i] = jnp.where(blockdiag, jnp.dot(kzf, vv, preferred_element_type=F32), 0.0)
        kvb_scr[pi, i] = jnp.where(blockdiag, jnp.dot(kzb, vv, preferred_element_type=F32), 0.0)

    def out_one(i, pi):
        rows = pl.ds(pl.multiple_of(i * C, C), C)
        lo, hi = pi * LANES, (pi + 1) * LANES
        q = q_ref[0, rows, lo:hi]
        k = k_ref[0, rows, lo:hi]
        vv = v_ref[0, rows, lo:hi]
        zero = jnp.zeros_like(q)
        s0 = lax.dot_general(jnp.where(first, q, zero), k, dn, preferred_element_type=F32)
        s1 = lax.dot_general(jnp.where(first, zero, q), k, dn, preferred_element_type=F32)
        p0 = (s0 * w_scr[2 * pi]).astype(BF16)
        p1 = (s1 * w_scr[2 * pi + 1]).astype(BF16)
        o = jnp.dot(p0, jnp.where(first, vv, zero), preferred_element_type=F32)
        o = o + jnp.dot(p1, jnp.where(first, zero, vv), preferred_element_type=F32)
        o = o + jnp.dot(q, sfs_scr[pi, i].astype(BF16), preferred_element_type=F32) * dec_scr[pi, 2]
        o = o + jnp.dot(q, sbs_scr[pi, i].astype(BF16), preferred_element_type=F32) * dec_scr[pi, 3]
        inv = 1.0 / RET_DV
        tot = jnp.sum(o, axis=-1, keepdims=True)
        m0 = jnp.sum(jnp.where(first, o, 0.0), axis=-1, keepdims=True)
        mu = jnp.where(first, m0, tot - m0) * inv
        dlt = o - mu
        d2 = dlt * dlt
        tot2 = jnp.sum(d2, axis=-1, keepdims=True)
        v0 = jnp.sum(jnp.where(first, d2, 0.0), axis=-1, keepdims=True)
        var = jnp.where(first, v0, tot2 - v0) * inv
        y = dlt * lax.rsqrt(var + EPS) * gng_ref[:, lo:hi]
        y_ref[0, rows, lo:hi] = (y * _silu(g_ref[0, rows, lo:hi])).astype(BF16)

    def over_chunks(fn):
        def body(t, carry):
            for u in range(U):
                for pi in range(P):
                    fn(t * U + u, pi)
            return carry
        lax.fori_loop(0, N // U, body, 0)

    over_chunks(sums_one)

    gc_f = [jnp.exp(lgf_ref[:, pi * LANES:(pi + 1) * LANES] * float(C)) for pi in range(P)]
    gc_b = [jnp.exp(lgb_ref[:, pi * LANES:(pi + 1) * LANES] * float(C)) for pi in range(P)]

    def fwd_scan(i, ss):
        for pi in range(P):
            sfs_scr[pi, i] = ss[pi]
        return tuple(ss[pi] * gc_f[pi] + kvf_scr[pi, i] for pi in range(P))

    def bwd_scan(ii, ss):
        i = N - 1 - ii
        for pi in range(P):
            sbs_scr[pi, i] = ss[pi]
        return tuple(ss[pi] * gc_b[pi] + kvb_scr[pi, i] for pi in range(P))

    sf = lax.fori_loop(0, N, fwd_scan, tuple(sf0_ref[0, pi] for pi in range(P)))
    sb = lax.fori_loop(0, N, bwd_scan, tuple(sb0_ref[0, pi] for pi in range(P)))
    for pi in range(P):
        sfo_ref[0, pi] = sf[pi]
        sbo_ref[0, pi] = sb[pi]

    over_chunks(out_one)


def _retention(lgd, q, k, v, g, gng, lgf, lgb, sf0, sb0):
    B, L, W = q.shape
    C = RET_CHUNK
    N = L // C
    P = RET_HEADS // 2
    seq = lambda: pl.BlockSpec((1, L, W), lambda b: (b, 0, 0))
    st = lambda: pl.BlockSpec((1, P, LANES, LANES), lambda b: (b, 0, 0, 0))
    st_shape = jax.ShapeDtypeStruct((B, P, LANES, LANES), F32)
    return pl.pallas_call(
        functools.partial(_ret_kernel, n_chunks=N),
        out_shape=[jax.ShapeDtypeStruct((B, L, W), BF16), st_shape, st_shape],
        grid=(B,),
        in_specs=[pl.BlockSpec(memory_space=pltpu.SMEM), seq(), seq(), seq(), seq(),
                  pl.BlockSpec((1, W), lambda b: (0, 0)), pl.BlockSpec((1, W), lambda b: (0, 0)),
                  pl.BlockSpec((1, W), lambda b: (0, 0)), st(), st()],
        out_specs=[seq(), st(), st()],
        scratch_shapes=[pltpu.VMEM((RET_HEADS, C, C), F32), pltpu.VMEM((P, 4, C, LANES), F32)]
        + [pltpu.VMEM((P, N, LANES, LANES), F32) for _ in range(4)],
        compiler_params=_params("arbitrary"),
        name="retention",
    )(lgd, q, k, v, g, gng, lgf, lgb, sf0, sb0)


def _attn_kernel(*refs, lengths):
    q_ref = refs[0]
    src = refs[1:1 + 3 * len(lengths)]
    o_ref, s_scr, p_scr, m_scr, a_scr, acc_scr = refs[1 + 3 * len(lengths):]
    tq = q_ref.shape[1]
    dn = (((1,), (1,)), ((), ()))
    H = MLA_HEADS
    RB = ATT_ROWS

    m_scr[...] = jnp.full(m_scr.shape, -1e30, F32)
    acc_scr[...] = jnp.zeros(acc_scr.shape, F32)

    def step(k_ref, ve_ref, vo_ref, rows, tk):
        def scores(h):
            q = q_ref[0, :, h * LANES:(h + 1) * LANES]
            k = k_ref[0, rows, h * LANES:(h + 1) * LANES]
            s_scr[h % ATT_SLOTS, :, 0:tk] = lax.dot_general(q, k, dn, preferred_element_type=F32)

        for h in range(ATT_AHEAD):
            scores(h)
        for h in range(H):
            if h + ATT_AHEAD < H:
                scores(h + ATT_AHEAD)
            for r0 in range(0, tq, RB):
                s = s_scr[h % ATT_SLOTS, r0:r0 + RB, 0:tk]
                m_old = m_scr[h, r0:r0 + RB, :]
                n = jnp.maximum(m_old, jnp.max(s, axis=-1, keepdims=True))
                p_scr[h % 2, r0:r0 + RB, 0:tk] = jnp.exp2(s - jnp.tile(n, (1, tk // LANES))).astype(BF16)
                a_scr[h % 2, r0:r0 + RB, :] = jnp.exp2(m_old - n)
                m_scr[h, r0:r0 + RB, :] = n
            v_ref = ve_ref if h % 2 == 0 else vo_ref
            v = v_ref[0, rows, (h // 2) * LANES:(h // 2 + 1) * LANES]
            acc_scr[h] = acc_scr[h] * a_scr[h % 2] + jnp.dot(p_scr[h % 2, :, 0:tk], v, preferred_element_type=F32)

    for si, length in enumerate(lengths):
        k_ref, ve_ref, vo_ref = src[3 * si:3 * si + 3]
        tk = min(ATT_TK, length)
        n = length // tk
        if n == 1:
            step(k_ref, ve_ref, vo_ref, pl.ds(0, tk), tk)
        else:
            @pl.loop(0, n)
            def _(c):
                step(k_ref, ve_ref, vo_ref, pl.ds(pl.multiple_of(c * tk, tk), tk), tk)

    first = lax.broadcasted_iota(jnp.int32, (tq, LANES), 1) < MLA_V
    for pj in range(H // 2):
        a0, a1 = acc_scr[2 * pj], acc_scr[2 * pj + 1]
        out = jnp.where(first, a0 / a0[:, MLA_V:MLA_V + 1], a1 / a1[:, 0:1])
        o_ref[0, :, pj * LANES:(pj + 1) * LANES] = out.astype(BF16)


def _attention(q, sources):
    B, L, _ = q.shape
    tq = min(ATT_TQ, L)
    lengths =tuple(s[0].shape[1] for s in sources)
    in_specs = [pl.BlockSpec((1, tq, MQ_W), lambda b, i: (b, i, 0))]
    args = [q]
    for k, ve, vo in sources:
        S = k.shape[1]
        in_specs += [pl.BlockSpec((1, S, MQ_W), lambda b, i: (b, 0, 0)),
                     pl.BlockSpec((1, S, 512), lambda b, i: (b, 0, 0)),
                     pl.BlockSpec((1, S, 512), lambda b, i: (b, 0, 0))]
        args += [k, ve, vo]
    return pl.pallas_call(
        functools.partial(_attn_kernel, lengths=lengths),
        out_shape=jax.ShapeDtypeStruct((B, L, MLA_HEADS * MLA_V), BF16),
        grid=(B, L // tq),
        in_specs=in_specs,
        out_specs=pl.BlockSpec((1, tq, MLA_HEADS * MLA_V), lambda b, i: (b, i, 0)),
        scratch_shapes=[pltpu.VMEM((ATT_SLOTS, tq, ATT_TK), F32),
                        pltpu.VMEM((2, tq, ATT_TK), BF16),
                        pltpu.VMEM((MLA_HEADS, tq, LANES), F32),
                        pltpu.VMEM((2, tq, LANES), F32),
                        pltpu.VMEM((MLA_HEADS, tq, LANES), F32)],
        compiler_params=_params("arbitrary", "arbitrary"),
        name="attention",
    )(*args)


def _mixer_kernel(a_ref, ap_ref, an_ref, cw_ref, cb_ref, lg_ref, lb_ref, yr_ref, at_ref, wo_ref,
                  pg_ref, gate_ref, x_ref, o_ref, win_scr, shf_scr, yc_scr):
    tm = a_ref.shape[1]
    i = pl.program_id(1)
    n = pl.num_programs(1)
    win_scr[0:HALO] = jnp.where(i > 0, ap_ref[0], 0.0)
    win_scr[HALO:HALO + tm] = a_ref[0]
    win_scr[HALO + tm:HALO + tm + HALO] = jnp.where(i < n - 1, an_ref[0], 0.0)
    span = tm + CONV_SPAN
    for ph in range(1, SUBLANES):
        shf_scr[ph - 1] = win_scr[ph:ph + span, :]
    sub = 64
    off = HALO - CONV_K // 2
    for r0 in range(0, tm, sub):
        acc = jnp.zeros((sub, CONV_CH), F32) + cb_ref[...]
        for t in range(CONV_K):
            ph = (off + t) % SUBLANES
            base = r0 + off + t - ph
            rows = win_scr[base:base + sub, :] if ph == 0 else shf_scr[ph - 1, base:base + sub, :]
            acc = acc + cw_ref[t:t + 1, :] * rows
        mu = jnp.mean(acc, axis=-1, keepdims=True)
        d = acc - mu
        var = jnp.mean(d * d, axis=-1, keepdims=True)
        yc = d * lax.rsqrt(var + EPS) * lg_ref[...] + lb_ref[...]
        yc_scr[r0:r0 + sub, :] = _silu(yc).astype(BF16)
    y = jnp.dot(yc_scr[...], wo_ref[0:256, :], preferred_element_type=F32)
    y = y + jnp.dot(yr_ref[0], wo_ref[256:512, :], preferred_element_type=F32)
    y = y + jnp.dot(at_ref[0], wo_ref[512:1024, :], preferred_element_type=F32)
    r = y * lax.rsqrt(jnp.mean(y * y, axis=-1, keepdims=True) + EPS) * pg_ref[...]
    o_ref[0] = x_ref[0] + gate_ref[0] * r


def _mixer(a, cw, cb, lg, lb, yr, att, wo, pg, gate, x):
    B, L, D = x.shape
    tm = ROW_TILE
    hb = tm // HALO
    nh = L // HALO
    row = lambda w: pl.BlockSpec((1, tm, w), lambda b, i: (b, i, 0))
    return pl.pallas_call(
        _mixer_kernel,
        out_shape=jax.ShapeDtypeStruct((B, L, D), F32),
        grid=(B, L // tm),
        in_specs=[row(CONV_CH),
                  pl.BlockSpec((1, HALO, CONV_CH), lambda b, i: (b, jnp.maximum(i * hb - 1, 0), 0)),
                  pl.BlockSpec((1, HALO, CONV_CH), lambda b, i: (b, jnp.minimum((i + 1) * hb, nh - 1), 0)),
                  _resident(cw.shape), _resident((1, CONV_CH)), _resident((1, CONV_CH)),
                  _resident((1, CONV_CH)), row(256), row(512), _resident(wo.shape),
                  _resident((1, D)), pl.BlockSpec((1, 1, D), lambda b, i: (b, 0, 0)), row(D)],
        out_specs=row(D),
        scratch_shapes=[pltpu.VMEM((tm + 2 * HALO, CONV_CH), F32),
                        pltpu.VMEM((SUBLANES - 1, tm + CONV_SPAN, CONV_CH), F32),
                        pltpu.VMEM((tm, CONV_CH), BF16)],
        compiler_params=_params("arbitrary", "arbitrary"),
        name="mixer_out",
    )(a, a, a, cw, cb, lg, lb, yr, att, wo, pg, gate, x)


def _ffn_kernel(x_ref, sh_ref, sc_ref, g_ref, w1_ref, w2_ref, pg_ref, gate_ref, o_ref):
    x = x_ref[0]
    h = x * lax.rsqrt(jnp.mean(x * x, axis=-1, keepdims=True) + EPS) * g_ref[...]
    h = (h * (1.0 + sc_ref[0]) + sh_ref[0]).astype(BF16)
    y = jnp.zeros(x.shape, F32)
    for lo, hi in FF_CHUNKS:
        u = jnp.dot(h, w1_ref[:, lo:hi], preferred_element_type=F32)
        gt = jnp.dot(h, w1_ref[:, D_FF + lo:D_FF + hi], preferred_element_type=F32)
        y = y + jnp.dot((_silu(gt) * u).astype(BF16), w2_ref[lo:hi, :], preferred_element_type=F32)
    r = y * lax.rsqrt(jnp.mean(y * y, axis=-1, keepdims=True) + EPS) * pg_ref[...]
    o_ref[0] = x + gate_ref[0] * r


def _ffn(x, sh, sc, g, w1, w2, pg, gate):
    B, L, D = x.shape
    tm = min(FFN_TILE, L)
    row = pl.BlockSpec((1, tm, D), lambda b, i: (b, i, 0))
    vec = pl.BlockSpec((1, 1, D), lambda b, i: (b, 0, 0))
    return pl.pallas_call(
        _ffn_kernel,
        out_shape=jax.ShapeDtypeStruct((B, L, D), F32),
        grid=(B, L // tm),
        in_specs=[row, vec, vec, _resident((1, D)), _resident(w1.shape), _resident(w2.shape),
                  _resident((1, D)), vec],
        out_specs=row,
        compiler_params=_params("arbitrary", "arbitrary"),
        name="ffn",
    )(x, sh, sc, g, w1, w2, pg, gate)


def _rot_cols(w, f):
    lead = w.shape[:-1]
    wr = w.reshape(lead + (-1, 2, f))
    return jnp.stack([-wr[..., 1, :], wr[..., 0, :]], axis=-2).reshape(w.shape)


def _rope_tables(T):
    t = jnp.arange(T, dtype=jnp.int32)
    rowp = (t // GRID_W).astype(F32)[:, None]
    colp = (t % GRID_W).astype(F32)[:, None]

    def half(f):
        inv = ROPE_BASE ** (-jnp.arange(f, dtype=F32) / f)
        ar, ac = rowp * inv[None, :], colp * inv[None, :]
        ang = jnp.concatenate([ar, ar, ac, ac], axis=-1)
        return jnp.cos(ang), jnp.sin(ang)

    cr, sr = half(RET_DK // 4)
    cosr, sinr = jnp.tile(cr, (1, 2)), jnp.tile(sr, (1, 2))
    cm, sm = half(MLA_ROPE // 4)
    pad = lambda a, fill: jnp.concatenate(
        [jnp.full((T, MLA_NOPE), fill, F32), a, jnp.full((T, LANES - MLA_NOPE - MLA_ROPE), fill, F32)], axis=-1)
    return cosr, sinr, pad(cm, 1.0), pad(sm, 0.0)


def _layer_weights(l, w_in, mla_w_uq, mla_w_ukv, w_out, ffn_w_in, ffn_w_out):
    wi = w_in[l]
    offs = [0]
    for s in (2 * CONV_CH, 256, 256, 256, 256, MLA_Q_RANK, MLA_KV_RANK, MLA_ROPE):
        offs.append(offs[-1] + s)
    wa, wq, wk, wv, wg, wcq, wckv, wkr = [wi[:, offs[i]:offs[i + 1]] for i in range(8)]
    wk = wk * (RET_DK ** -0.5)
    D = wi.shape[0]
    zl = jnp.zeros((D, MLA_NOPE), F32)
    zr = jnp.zeros((D, LANES - MLA_NOPE - MLA_ROPE), F32)
    w_ext = jnp.concatenate(
        [wa, wq, _rot_cols(wq, RET_DK // 4), wk, _rot_cols(wk, RET_DK // 4), wv, wg, wcq, wckv,
         zl, wkr, zr, zl, _rot_cols(wkr, MLA_ROPE // 4), zr], axis=1).astype(BF16)

    uq = mla_w_uq[l].reshape(MLA_Q_RANK, MLA_HEADS, MLA_NOPE + MLA_ROPE)
    uq_rope = uq[:, :, MLA_NOPE:]
    zq = jnp.zeros((MLA_Q_RANK, MLA_HEADS, LANES - MLA_NOPE - MLA_ROPE), F32)
    uq_pad = jnp.concatenate([uq, zq], axis=-1).reshape(MLA_Q_RANK, MQ_W)
    uq_rot = jnp.concatenate([jnp.zeros_like(uq[:, :, :MLA_NOPE]), _rot_cols(uq_rope, MLA_ROPE // 4), zq],
                             axis=-1).reshape(MLA_Q_RANK, MQ_W)
    wuq = jnp.concatenate([uq_pad, uq_rot], axis=1).astype(BF16)

    ukv = mla_w_ukv[l].reshape(MLA_KV_RANK, MLA_HEADS, MLA_NOPE + MLA_V)
    uk_pad = jnp.concatenate([ukv[:, :, :MLA_NOPE], jnp.zeros((MLA_KV_RANK, MLA_HEADS, LANES - MLA_NOPE), F32)],
                             axis=-1).reshape(MLA_KV_RANK, MQ_W)
    uv = ukv[:, :, MLA_NOPE:].reshape(MLA_KV_RANK, MLA_HEADS * MLA_V)
    wukv = jnp.concatenate([uk_pad, uv], axis=1).astype(BF16)
    return w_ext, wuq, wukv, w_out[l].astype(BF16), ffn_w_in[l].astype(BF16), ffn_w_out[l].astype(BF16)


def kernel(x, c, ctx, c_ctx, mod_w, mod_b, pre1_g, post1_g, pre2_g, post2_g, w_in, conv_w, conv_b,
           conv_ln_g, conv_ln_b, ret_log_decay, ret_gn_g, mla_q_norm_g, mla_w_uq, mla_kv_norm_g,
           mla_w_ukv, w_out, ffn_w_in, ffn_w_out):
    B, T, D = x.shape
    TC = ctx.shape[1]
    cv = jnp.concatenate([c, c_ctx[None], jnp.zeros((8 - B - 1, D), F32)], axis=0)
    mods = _modulation(cv, mod_w, mod_b).reshape(DEPTH, 8, 6, 1, D)

    cosr, sinr, cosm, sinm = _rope_tables(T)
    ones = jnp.ones((TC, LANES), F32)
    zeros = jnp.zeros((TC, LANES), F32)
    zstate = jnp.zeros((B, RET_HEADS // 2, LANES, LANES), F32)

    xc = ctx
    for l in range(DEPTH):
        last = l == DEPTH - 1
        w_ext, wuq, wukv, wo, w1, w2 = _layer_weights(l, w_in, mla_w_uq, mla_w_ukv, w_out, ffn_w_in, ffn_w_out)
        lat = [mods[l, :B, j] for j in range(6)]
        cx = [jnp.broadcast_to(mods[l, B, j][None], (B, 1, D)) for j in range(6)]
        vec = lambda a: a[l][None]
        lgd = ret_log_decay[l]
        lgf = jnp.repeat(lgd[0], RET_DK)[None]
        lgb = jnp.repeat(lgd[1], RET_DK)[None]
        cw = jnp.concatenate([conv_w[l], jnp.zeros((1, CONV_CH), F32)], axis=0)

        proj = lambda xx, md, tabs: _inproj(xx, md[0], md[1], vec(pre1_g), w_ext, vec(mla_q_norm_g), wuq,
                                            vec(mla_kv_norm_g), wukv, *tabs)
        aL, rqL, rkL, rvL, rgL, mqL, mkL, mveL, mvoL = proj(x, lat, (cosr, sinr, cosm, sinm))
        aC, rqC, rkC, rvC, rgC, mqC, mkC, mveC, mvoC = proj(xc, cx, (ones, zeros, ones, zeros))

        yrC, sf, sb = _retention(lgd, rqC, rkC, rvC, rgC, vec(ret_gn_g), lgf, lgb, zstate, zstate)
        yrL, _, _ = _retention(lgd, rqL, rkL, rvL, rgL, vec(ret_gn_g), lgf, lgb, sf, sb)

        attL = _attention(mqL, [(mkC, mveC, mvoC), (mkL, mveL, mvoL)])
        mix = lambda a, yr, att, md, xx: _mixer(a, cw, vec(conv_b), vec(conv_ln_g), vec(conv_ln_b), yr, att, wo,
                                                vec(post1_g), md[2], xx)
        ffn = lambda xx, md: _ffn(xx, md[3], md[4], vec(pre2_g), w1, w2, vec(post2_g), md[5])
        x = mix(aL, yrL, attL, lat, x)
        if not last:
            attC = _attention(mqC, [(mkC, mveC, mvoC)])
            xc = mix(aC, yrC, attC, cx, xc)
        x = ffn(x, lat)
        if not last:
            xc = ffn(xc, cx)
    return x
```

```python
import functools

import numpy as np
import jax
import jax.numpy as jnp
from jax import lax
from jax.experimental import pallas as pl
from jax.experimental.pallas import tpu as pltpu

F32 = jnp.float32
BF16 = jnp.bfloat16

D_MODEL = 1024
DEPTH = 2
GRID_W = 64
EPS = 1e-6
LOG2E = 1.4426950408889634
ROPE_BASE = 10000.0
CONV_CH = 256
CONV_K = 31
RET_HEADS = 4
RET_DK = 64
RET_DV = 64
MLA_HEADS = 8
MLA_Q_RANK = 256
MLA_KV_RANK = 128
MLA_NOPE = 64
MLA_ROPE = 32
MLA_V = 64
D_FF = 2816

LANES = 128
SUBLANES = 8
HALO = 16
CONV_SPAN = 24
ROW_TILE = 256
FFN_TILE = 512
RET_CHUNK = 256
ATT_TQ = 512
ATT_TK = 1024
ATT_AHEAD = 2
ATT_SLOTS = ATT_AHEAD + 1
ATT_ROWS = 32
MXU_DIM = 256
FF_CHUNKS = ((0, 6 * MXU_DIM), (6 * MXU_DIM, D_FF))
VMEM_LIMIT = 56 * 1024 * 1024

_OFF_A = 0
_OFF_Q = 512
_OFF_K = 768
_OFF_V = 1024
_OFF_G = 1280
_OFF_CQ = 1536
_OFF_CKV = 1792
_OFF_KRP = 1920
D_EXT = 2048
MOD_ROWS = 8
MQ_W = MLA_HEADS * LANES


def _sigmoid(x):
    return 1.0 / (1.0 + jnp.exp(-x))


def _silu(x):
    return x * _sigmoid(x)


def _params(*sem):
    return pltpu.CompilerParams(dimension_semantics=sem, vmem_limit_bytes=VMEM_LIMIT)


class _Layered:
    def __init__(self, arr, layer):
        self.arr, self.layer = arr, layer

    @property
    def spec(self):
        n, layer = self.arr.ndim - 1, self.layer
        return pl.BlockSpec((None,) + self.arr.shape[1:], lambda *_: (layer,) + (0,) * n,
                            pipeline_mode=pl.Buffered(1))


class _ModRows:
    def __init__(self, table, base, per_batch):
        self.table, self.base, self.per_batch = table, base, per_batch

    def spec(self, j):
        base, step = self.base + j, self.per_batch
        return pl.BlockSpec((1, 1, self.table.shape[-1]), lambda b, i: (base + step * b, 0, 0))


def _mod_kernel(cv_ref, w_ref, b_ref, o_ref):
    s = _silu(cv_ref[...])
    o_ref[0] = jnp.dot(s.astype(BF16), w_ref[0].astype(BF16), preferred_element_type=F32) + b_ref[0]


def _modulation(cv, mod_w, mod_b):
    L, D, N = mod_w.shape
    tn = 1536
    return pl.pallas_call(
        _mod_kernel,
        out_shape=jax.ShapeDtypeStruct((L, 8, N), F32),
        grid=(L, N // tn),
        in_specs=[pl.BlockSpec((8, D), lambda l, j: (0, 0)),
                  pl.BlockSpec((1, D, tn), lambda l, j: (l, 0, j)),
                  pl.BlockSpec((1, 1, tn), lambda l, j: (l, 0, j))],
        out_specs=pl.BlockSpec((1, 8, tn), lambda l, j: (l, 0, j)),
        compiler_params=_params("arbitrary", "arbitrary"),
        name="modulation",
    )(cv, mod_w, mod_b.reshape(L, 1, N))


def _rope(x, cos, sin_up, sin_dn, f):
    return x * cos + pltpu.roll(x, LANES - f, 1) * sin_up + pltpu.roll(x, f, 1) * sin_dn


def _inproj_kernel(x_ref, sh_ref, sc_ref, g_ref, w_ref, qng_ref, wuq_ref, kvg_ref, wukv_ref, tab_ref,
                   a_ref, rq_ref, rk_ref, rv_ref, rg_ref, mq_ref, mk_ref, mve_ref, mvo_ref):
    x = x_ref[0]
    h = x * lax.rsqrt(jnp.mean(x * x, axis=-1, keepdims=True) + EPS) * g_ref[...]
    hb = (h * (1.0 + sc_ref[0]) + sh_ref[0]).astype(BF16)

    def proj(lo, hi):
        return jnp.dot(hb, w_ref[:, lo:hi], preferred_element_type=F32)

    pm = proj(_OFF_CQ, D_EXT)
    pr = proj(_OFF_Q, _OFF_V)
    cq = pm[:, 0:MLA_Q_RANK]
    qn = cq * lax.rsqrt(jnp.mean(cq * cq, axis=-1, keepdims=True) + EPS) * qng_ref[...]
    qq = jnp.dot(qn.astype(BF16), wuq_ref[...], preferred_element_type=F32)
    ckv = pm[:, _OFF_CKV - _OFF_CQ:_OFF_KRP - _OFF_CQ]
    kvn = ckv * lax.rsqrt(jnp.mean(ckv * ckv, axis=-1, keepdims=True) + EPS) * kvg_ref[...]
    kv = jnp.dot(kvn.astype(BF16), wukv_ref[...], preferred_element_type=F32)
    pa = proj(_OFF_A, _OFF_Q)
    pv = proj(_OFF_V, _OFF_CQ)

    a_ref[0] = pa[:, 0:CONV_CH] * _sigmoid(pa[:, CONV_CH:2 * CONV_CH])

    ret_tabs = (tab_ref[0], tab_ref[1], tab_ref[2], RET_DK // 4)
    for i in range(2):
        lo, hi = i * LANES, (i + 1) * LANES
        rq_ref[0, :, lo:hi] = _rope(pr[:, lo:hi], *ret_tabs).astype(BF16)
        rk_ref[0, :, lo:hi] = _rope(pr[:, 256 + lo:256 + hi], *ret_tabs).astype(BF16)
    rv_ref[0] = pv[:, 0:256].astype(BF16)
    rg_ref[0] = pv[:, 256:512]

    mla_tabs = (tab_ref[3], tab_ref[4], tab_ref[5], MLA_ROPE // 4)
    scale = float((MLA_NOPE + MLA_ROPE) ** -0.5 * LOG2E)
    kr_slot = _rope(pm[:, _OFF_KRP - _OFF_CQ:D_EXT - _OFF_CQ], *mla_tabs)
    for hh in range(MLA_HEADS):
        lo, hi = hh * LANES, (hh + 1) * LANES
        mq_ref[0, :, lo:hi] = (_rope(qq[:, lo:hi], *mla_tabs) * scale).astype(BF16)
        mk_ref[0, :, lo:hi] = (kv[:, lo:hi] + kr_slot).astype(BF16)
    v = kv[:, MQ_W:]
    lane = lax.broadcasted_iota(jnp.int32, v.shape, 1)
    lane = lane % LANES
    even = lane < MLA_V
    mve_ref[0] = jnp.where(even, v, jnp.where(lane == MLA_V, 1.0, 0.0)).astype(BF16)
    mvo_ref[0] = jnp.where(even, jnp.where(lane == 0, 1.0, 0.0), v).astype(BF16)


def _inproj(x, mod, g, w_ext, qng, wuq, kvg, wukv, tabs):
    B, L, D = x.shape
    tm = ROW_TILE
    row = lambda w: pl.BlockSpec((1, tm, w), lambda b, i: (b, i, 0))
    outs = [(CONV_CH, F32), (256, BF16), (256, BF16), (256, BF16), (256, F32),
            (MQ_W, BF16), (MQ_W, BF16), (512, BF16), (512, BF16)]
    layered = [g, w_ext, qng, wuq, kvg, wukv]
    return pl.pallas_call(
        _inproj_kernel,
        out_shape=[jax.ShapeDtypeStruct((B, L, w), dt) for w, dt in outs],
        grid=(B, L // tm),
        in_specs=[row(D), mod.spec(0), mod.spec(1)] + [a.spec for a in layered]
        + [pl.BlockSpec((tabs.shape[0], tm, LANES), lambda b, i: (0, i, 0))],
        out_specs=[row(w) for w, _ in outs],
        compiler_params=_params("arbitrary", "arbitrary"),
        name="inproj",
    )(x, mod.table, mod.table, *[a.arr for a in layered], tabs)


def _ret_kernel(lgd_ref, q_ref, k_ref, v_ref, g_ref, gng_ref, lgf_ref, lgb_ref, sf0_ref, sb0_ref,
                y_ref, sfo_ref, sbo_ref, w_scr, dec_scr, kvf_scr, kvb_scr, sfs_scr, sbs_scr, *, n_chunks):
    C = RET_CHUNK
    N = n_chunks
    P = RET_HEADS // 2
    U = 2 if N % 2 == 0 else 1

    @pl.when(pl.program_id(0) == 0)
    def _():
        t = lax.broadcasted_iota(jnp.int32, (C, C), 0)
        m = lax.broadcasted_iota(jnp.int32, (C, C), 1)
        d = (t - m).astype(F32)
        for hh in range(RET_HEADS):
            w_scr[hh] = jnp.exp(jnp.where(d >= 0.0, lgd_ref[0, hh] * d, lgd_ref[1, hh] * (-d)))
        j = lax.broadcasted_iota(jnp.int32, (C, LANES), 0).astype(F32)
        for pi in range(P):
            lgf = lgf_ref[:, pi * LANES:(pi + 1) * LANES]
            lgb = lgb_ref[:, pi * LANES:(pi + 1) * LANES]
            dec_scr[pi, 0] = jnp.exp(lgf * (C - 1.0 - j))
            dec_scr[pi, 1] = jnp.exp(lgb * j)
            dec_scr[pi, 2] = jnp.exp(lgf * (j + 1.0))
            dec_scr[pi, 3] = jnp.exp(lgb * (C - j))

    first = lax.broadcasted_iota(jnp.int32, (C, LANES), 1) < RET_DV
    r128 = lax.broadcasted_iota(jnp.int32, (LANES, LANES), 0)
    c128 = lax.broadcasted_iota(jnp.int32, (LANES, LANES), 1)
    blockdiag = (r128 < RET_DK) == (c128 < RET_DV)
    dn = (((1,), (1,)), ((), ()))

    def sums_one(i, pi):
        rows = pl.ds(pl.multiple_of(i * C, C), C)
        lo, hi = pi * LANES, (pi + 1) * LANES
        kf = k_ref[0, rows, lo:hi].astype(F32)
        vv = v_ref[0, rows, lo:hi]
        kzf = (kf * dec_scr[pi, 0]).T.astype(BF16)
        kzb = (kf * dec_scr[pi, 1]).T.astype(BF16)
        kvf_scr[pi, i] = jnp.where(blockdiag, jnp.dot(kzf, vv, preferred_element_type=F32), 0.0)
        kvb_scr[pi, i] = jnp.where(blockdiag, jnp.dot(kzb, vv, preferred_element_type=F32), 0.0)

    def out_one(i, pi):
        rows = pl.ds(pl.multiple_of(i * C, C), C)
        lo, hi = pi * LANES, (pi + 1) * LANES
        q = q_ref[0, rows, lo:hi]
        k = k_ref[0, rows, lo:hi]
        vv = v_ref[0, rows, lo:hi]
        zero = jnp.zeros_like(q)
        s0 = lax.dot_general(jnp.where(first, q, zero), k, dn, preferred_element_type=F32)
        s1 = lax.dot_general(jnp.where(first, zero, q), k, dn, preferred_element_type=F32)
        p0 = (s0 * w_scr[2 * pi]).astype(BF16)
        p1 = (s1 * w_scr[2 * pi + 1]).astype(BF16)
        o = jnp.dot(p0, jnp.where(first, vv, zero), preferred_element_type=F32)
        o = o + jnp.dot(p1, jnp.where(first, zero, vv), preferred_element_type=F32)
        o = o + jnp.dot(q, sfs_scr[pi, i].astype(BF16), preferred_element_type=F32) * dec_scr[pi, 2]
        o = o + jnp.dot(q, sbs_scr[pi, i].astype(BF16), preferred_element_type=F32) * dec_scr[pi, 3]
        inv = 1.0 / RET_DV
        tot = jnp.sum(o, axis=-1, keepdims=True)
        m0 = jnp.sum(jnp.where(first, o, 0.0), axis=-1, keepdims=True)
        mu = jnp.where(first, m0, tot - m0) * inv
        dlt = o - mu
        d2 = dlt * dlt
        tot2 = jnp.sum(d2, axis=-1, keepdims=True)
        v0 = jnp.sum(jnp.where(first, d2, 0.0), axis=-1, keepdims=True)
        var = jnp.where(first, v0, tot2 - v0) * inv
        y = dlt * lax.rsqrt(var + EPS) * gng_ref[:, lo:hi]
        y_ref[0, rows, lo:hi] = (y * _silu(g_ref[0, rows, lo:hi])).astype(BF16)

    def over_chunks(fn):
        def body(t, carry):
            for u in range(U):
                for pi in range(P):
                    fn(t * U + u, pi)
            return carry
        lax.fori_loop(0, N // U, body, 0)

    over_chunks(sums_one)

    gc_f = [jnp.exp(lgf_ref[:, pi * LANES:(pi + 1) * LANES] * float(C)) for pi in range(P)]
    gc_b = [jnp.exp(lgb_ref[:, pi * LANES:(pi + 1) * LANES] * float(C)) for pi in range(P)]

    def fwd_scan(i, ss):
        for pi in range(P):
            sfs_scr[pi, i] = ss[pi]
        return tuple(ss[pi] * gc_f[pi] + kvf_scr[pi, i] for pi in range(P))

    def bwd_scan(ii, ss):
        i = N - 1 - ii
        for pi in range(P):
            sbs_scr[pi, i] = ss[pi]
        return tuple(ss[pi] * gc_b[pi] + kvb_scr[pi, i] for pi in range(P))

    sf = lax.fori_loop(0, N, fwd_scan, tuple(sf0_ref[0, pi] for pi in range(P)))
    sb = lax.fori_loop(0, N, bwd_scan, tuple(sb0_ref[0, pi] for pi in range(P)))
    for pi in range(P):
        sfo_ref[0, pi] = sf[pi]
        sbo_ref[0, pi] = sb[pi]

    over_chunks(out_one)


def _retention(lgd, q, k, v, g, gng, lgf, lgb, sf0, sb0):
    B, L, W = q.shape
    C = RET_CHUNK
    N = L // C
    P = RET_HEADS // 2
    seq = lambda: pl.BlockSpec((1, L, W), lambda b: (b, 0, 0))
    st = lambda: pl.BlockSpec((1, P, LANES, LANES), lambda b: (b, 0, 0, 0))
    st_shape = jax.ShapeDtypeStruct((B, P, LANES, LANES), F32)
    return pl.pallas_call(
        functools.partial(_ret_kernel, n_chunks=N),
        out_shape=[jax.ShapeDtypeStruct((B, L, W), BF16), st_shape, st_shape],
        grid=(B,),
        in_specs=[pl.BlockSpec(memory_space=pltpu.SMEM), seq(), seq(), seq(), seq(),
                  gng.spec, lgf.spec, lgb.spec, st(), st()],
        out_specs=[seq(), st(), st()],
        scratch_shapes=[pltpu.VMEM((RET_HEADS, C, C), F32), pltpu.VMEM((P, 4, C, LANES), F32)]
        + [pltpu.VMEM((P, N, LANES, LANES), F32) for _ in range(4)],
        compiler_params=_params("arbitrary"),
        name="retention",
    )(lgd, q, k, v, g, gng.arr, lgf.arr, lgb.arr, sf0, sb0)


def _attn_kernel(*refs, lengths):
    q_ref = refs[0]
    src = refs[1:1 + 3 * len(lengths)]
    o_ref, s_scr, p_scr, m_scr, a_scr, acc_scr = refs[1 + 3 * len(lengths):]
    tq = q_ref.shape[1]
    dn = (((1,), (1,)), ((), ()))
    H = MLA_HEADS
    RB = ATT_ROWS

    m_scr[...] = jnp.full(m_scr.shape, -1e30, F32)
    acc_scr[...] = jnp.zeros(acc_scr.shape, F32)

    def step(k_ref, ve_ref, vo_ref, rows, tk):
        def scores(h):
            q = q_ref[0, :, h * LANES:(h + 1) * LANES]
            k = k_ref[0, rows, h * LANES:(h + 1) * LANES]
            s_scr[h % ATT_SLOTS, :, 0:tk] = lax.dot_general(q, k, dn, preferred_element_type=F32)

        for h in range(ATT_AHEAD):
            scores(h)
        for h in range(H):
            if h + ATT_AHEAD < H:
                scores(h + ATT_AHEAD)
            for r0 in range(0, tq, RB):
                s = s_scr[h % ATT_SLOTS, r0:r0 + RB, 0:tk]
                m_old = m_scr[h, r0:r0 + RB, :]
                n = jnp.maximum(m_old, jnp.max(s, axis=-1, keepdims=True))
                p_scr[h % 2, r0:r0 + RB, 0:tk] = jnp.exp2(s - jnp.tile(n, (1, tk // LANES))).astype(BF16)
                a_scr[h % 2, r0:r0 + RB, :] = jnp.exp2(m_old - n)
                m_scr[h, r0:r0 + RB, :] = n
            v_ref = ve_ref if h % 2 == 0 else vo_ref
            v = v_ref[0, rows, (h // 2) * LANES:(h // 2 + 1) * LANES]
            acc_scr[h] = acc_scr[h] * a_scr[h % 2] + jnp.dot(p_scr[h % 2, :, 0:tk], v, preferred_element_type=F32)

    for si, length in enumerate(lengths):
        k_ref, ve_ref, vo_ref = src[3 * si:3 * si + 3]
        tk = min(ATT_TK, length)
        n = length // tk
        if n == 1:
            step(k_ref, ve_ref, vo_ref, pl.ds(0, tk), tk)
        else:
            @pl.loop(0, n)
            def _(c):
                step(k_ref, ve_ref, vo_ref, pl.ds(pl.multiple_of(c * tk, tk), tk), tk)

    first = lax.broadcasted_iota(jnp.int32, (tq, LANES), 1) < MLA_V
    for pj in range(H // 2):
        a0, a1 = acc_scr[2 * pj], acc_scr[2 * pj + 1]
        out = jnp.where(first, a0 / a0[:, MLA_V:MLA_V + 1], a1 / a1[:, 0:1])
        o_ref[0, :, pj * LANES:(pj + 1) * LANES] = out.astype(BF16)


def _attention(q, sources):
    B, L, _ = q.shape
    tq = min(ATT_TQ, L)
    lengths =tuple(s[0].shape[1] for s in sources)
    in_specs = [pl.BlockSpec((1, tq, MQ_W), lambda b, i: (b, i, 0))]
    args = [q]
    for k, ve, vo in sources:
        S = k.shape[1]
        in_specs += [pl.BlockSpec((1, S, MQ_W), lambda b, i: (b, 0, 0)),
                     pl.BlockSpec((1, S, 512), lambda b, i: (b, 0, 0)),
                     pl.BlockSpec((1, S, 512), lambda b, i: (b, 0, 0))]
        args += [k, ve, vo]
    return pl.pallas_call(
        functools.partial(_attn_kernel, lengths=lengths),
        out_shape=jax.ShapeDtypeStruct((B, L, MLA_HEADS * MLA_V), BF16),
        grid=(B, L // tq),
        in_specs=in_specs,
        out_specs=pl.BlockSpec((1, tq, MLA_HEADS * MLA_V), lambda b, i: (b, i, 0)),
        scratch_shapes=[pltpu.VMEM((ATT_SLOTS, tq, ATT_TK), F32),
                        pltpu.VMEM((2, tq, ATT_TK), BF16),
                        pltpu.VMEM((MLA_HEADS, tq, LANES), F32),
                        pltpu.VMEM((2, tq, LANES), F32),
                        pltpu.VMEM((MLA_HEADS, tq, LANES), F32)],
        compiler_params=_params("arbitrary", "arbitrary"),
        name="attention",
    )(*args)


def _mixer_kernel(a_ref, ap_ref, an_ref, cw_ref, cb_ref, lg_ref, lb_ref, yr_ref, at_ref, wo_ref,
                  pg_ref, gate_ref, x_ref, o_ref, win_scr, shf_scr, yc_scr):
    tm = a_ref.shape[1]
    i = pl.program_id(1)
    n = pl.num_programs(1)
    win_scr[0:HALO] = jnp.where(i > 0, ap_ref[0], 0.0)
    win_scr[HALO:HALO + tm] = a_ref[0]
    win_scr[HALO + tm:HALO + tm + HALO] = jnp.where(i < n - 1, an_ref[0], 0.0)
    span = tm + CONV_SPAN
    for ph in range(1, SUBLANES):
        shf_scr[ph - 1] = win_scr[ph:ph + span, :]
    sub = 64
    off = HALO - CONV_K // 2
    for r0 in range(0, tm, sub):
        acc = jnp.zeros((sub, CONV_CH), F32) + cb_ref[...]
        for t in range(CONV_K):
            ph = (off + t) % SUBLANES
            base = r0 + off + t - ph
            rows = win_scr[base:base + sub, :] if ph == 0 else shf_scr[ph - 1, base:base + sub, :]
            acc = acc + cw_ref[t:t + 1, :] * rows
        mu = jnp.mean(acc, axis=-1, keepdims=True)
        d = acc - mu
        var = jnp.mean(d * d, axis=-1, keepdims=True)
        yc = d * lax.rsqrt(var + EPS) * lg_ref[...] + lb_ref[...]
        yc_scr[r0:r0 + sub, :] = _silu(yc).astype(BF16)
    y = jnp.dot(yc_scr[...], wo_ref[0:256, :], preferred_element_type=F32)
    y = y + jnp.dot(yr_ref[0], wo_ref[256:512, :], preferred_element_type=F32)
    y = y + jnp.dot(at_ref[0], wo_ref[512:1024, :], preferred_element_type=F32)
    r = y * lax.rsqrt(jnp.mean(y * y, axis=-1, keepdims=True) + EPS) * pg_ref[...]
    o_ref[0] = x_ref[0] + gate_ref[0] * r


def _mixer(a, cw, cb, lg, lb, yr, att, wo, pg, mod, x):
    B, L, D = x.shape
    tm = ROW_TILE
    hb = tm // HALO
    nh = L // HALO
    row = lambda w: pl.BlockSpec((1, tm, w), lambda b, i: (b, i, 0))
    return pl.pallas_call(
        _mixer_kernel,
        out_shape=jax.ShapeDtypeStruct((B, L, D), F32),
        grid=(B, L // tm),
        in_specs=[row(CONV_CH),
                  pl.BlockSpec((1, HALO, CONV_CH), lambda b, i: (b, jnp.maximum(i * hb - 1, 0), 0)),
                  pl.BlockSpec((1, HALO, CONV_CH), lambda b, i: (b, jnp.minimum((i + 1) * hb, nh - 1), 0)),
                  cw.spec, cb.spec, lg.spec, lb.spec, row(256), row(512), wo.spec, pg.spec, mod.spec(2), row(D)],
        out_specs=row(D),
        scratch_shapes=[pltpu.VMEM((tm + 2 * HALO, CONV_CH), F32),
                        pltpu.VMEM((SUBLANES - 1, tm + CONV_SPAN, CONV_CH), F32),
                        pltpu.VMEM((tm, CONV_CH), BF16)],
        compiler_params=_params("arbitrary", "arbitrary"),
        name="mixer_out",
    )(a, a, a, cw.arr, cb.arr, lg.arr, lb.arr, yr, att, wo.arr, pg.arr, mod.table, x)


def _ffn_kernel(x_ref, sh_ref, sc_ref, g_ref, w1_ref, w2_ref, pg_ref, gate_ref, o_ref):
    x = x_ref[0]
    h = x * lax.rsqrt(jnp.mean(x * x, axis=-1, keepdims=True) + EPS) * g_ref[...]
    h = (h * (1.0 + sc_ref[0]) + sh_ref[0]).astype(BF16)
    y = jnp.zeros(x.shape, F32)
    for lo, hi in FF_CHUNKS:
        u = jnp.dot(h, w1_ref[:, lo:hi], preferred_element_type=F32)
        gt = jnp.dot(h, w1_ref[:, D_FF + lo:D_FF + hi], preferred_element_type=F32)
        y = y + jnp.dot((_silu(gt) * u).astype(BF16), w2_ref[lo:hi, :], preferred_element_type=F32)
    r = y * lax.rsqrt(jnp.mean(y * y, axis=-1, keepdims=True) + EPS) * pg_ref[...]
    o_ref[0] = x + gate_ref[0] * r


def _ffn(x, mod, g, w1, w2, pg):
    B, L, D = x.shape
    tm = min(FFN_TILE, L)
    row = pl.BlockSpec((1, tm, D), lambda b, i: (b, i, 0))
    return pl.pallas_call(
        _ffn_kernel,
        out_shape=jax.ShapeDtypeStruct((B, L, D), F32),
        grid=(B, L // tm),
        in_specs=[row, mod.spec(3), mod.spec(4), g.spec, w1.spec, w2.spec, pg.spec, mod.spec(5)],
        out_specs=row,
        compiler_params=_params("arbitrary", "arbitrary"),
        name="ffn",
    )(x, mod.table, mod.table, g.arr, w1.arr, w2.arr, pg.arr, mod.table)


def _rope_tables(T, rotary):
    t = np.arange(T)
    rowp = (t // GRID_W).astype(np.float64)[:, None]
    colp = (t % GRID_W).astype(np.float64)[:, None]

    def half(f):
        inv = ROPE_BASE ** (-np.arange(f, dtype=np.float64) / f)
        ar, ac = rowp * inv[None, :], colp * inv[None, :]
        cos = np.concatenate([np.cos(ar)] * 2 + [np.cos(ac)] * 2, axis=-1)
        sin_r, sin_c, z = np.sin(ar), np.sin(ac), np.zeros_like(ar)
        sin_up = np.concatenate([-sin_r, z, -sin_c, z], axis=-1)
        sin_dn = np.concatenate([z, sin_r, z, sin_c], axis=-1)
        if not rotary:
            cos, sin_up, sin_dn = np.ones_like(cos), np.zeros_like(cos), np.zeros_like(cos)
        return cos, sin_up, sin_dn

    ret = [np.tile(a, (1, 2)) for a in half(RET_DK // 4)]

    def pad(a, fill):
        return np.concatenate([np.full((T, MLA_NOPE), fill), a,
                               np.full((T, LANES - MLA_NOPE - MLA_ROPE), fill)], axis=-1)

    cm, su, sd = half(MLA_ROPE // 4)
    return jnp.asarray(np.stack(ret + [pad(cm, 1.0), pad(su, 0.0), pad(sd, 0.0)]).astype(np.float32))


def _stacked_weights(w_in, mla_w_uq, mla_w_ukv, w_out, ffn_w_in, ffn_w_out):
    L, D, _ = w_in.shape
    offs = [0]
    for s in (2 * CONV_CH, 256, 256, 256, 256, MLA_Q_RANK, MLA_KV_RANK, MLA_ROPE):
        offs.append(offs[-1] + s)
    wa, wq, wk, wv, wg, wcq, wckv, wkr = [w_in[:, :, offs[i]:offs[i + 1]] for i in range(8)]
    zl = jnp.zeros((L, D, MLA_NOPE), F32)
    zr = jnp.zeros((L, D, LANES - MLA_NOPE - MLA_ROPE), F32)
    w_ext = jnp.concatenate([wa, wq, wk * (RET_DK ** -0.5), wv, wg, wcq, wckv, zl, wkr, zr], axis=2).astype(BF16)

    uq = mla_w_uq.reshape(L, MLA_Q_RANK, MLA_HEADS, MLA_NOPE + MLA_ROPE)
    zq = jnp.zeros((L, MLA_Q_RANK, MLA_HEADS, LANES - MLA_NOPE - MLA_ROPE), F32)
    wuq = jnp.concatenate([uq, zq], axis=-1).reshape(L, MLA_Q_RANK, MQ_W).astype(BF16)

    ukv = mla_w_ukv.reshape(L, MLA_KV_RANK, MLA_HEADS, MLA_NOPE + MLA_V)
    zk = jnp.zeros((L, MLA_KV_RANK, MLA_HEADS, LANES - MLA_NOPE), F32)
    uk_pad = jnp.concatenate([ukv[..., :MLA_NOPE], zk], axis=-1).reshape(L, MLA_KV_RANK, MQ_W)
    uv = ukv[..., MLA_NOPE:].reshape(L, MLA_KV_RANK, MLA_HEADS * MLA_V)
    wukv = jnp.concatenate([uk_pad, uv], axis=2).astype(BF16)
    return w_ext, wuq, wukv, w_out.astype(BF16), ffn_w_in.astype(BF16), ffn_w_out.astype(BF16)


def kernel(x, c, ctx, c_ctx, mod_w, mod_b, pre1_g, post1_g, pre2_g, post2_g, w_in, conv_w, conv_b,
           conv_ln_g, conv_ln_b, ret_log_decay, ret_gn_g, mla_q_norm_g, mla_w_uq, mla_kv_norm_g,
           mla_w_ukv, w_out, ffn_w_in, ffn_w_out):
    B, T, D = x.shape
    TC = ctx.shape[1]
    cv = jnp.concatenate([c, c_ctx[None], jnp.zeros((MOD_ROWS - B - 1, D), F32)], axis=0)
    mod_table = _modulation(cv, mod_w, mod_b).reshape(DEPTH * MOD_ROWS * 6, 1, D)

    tabs_lat = _rope_tables(T, True)
    tabs_ctx = _rope_tables(TC, False)
    zstate = jnp.zeros((B, RET_HEADS // 2, LANES, LANES), F32)

    stacks = _stacked_weights(w_in, mla_w_uq, mla_w_ukv, w_out, ffn_w_in, ffn_w_out)
    vec = lambda a: a.reshape(DEPTH, 1, -1)
    lg_lanes = jnp.repeat(ret_log_decay, RET_DK, axis=-1)
    stacks += (vec(pre1_g), vec(post1_g), vec(pre2_g), vec(post2_g), vec(mla_q_norm_g), vec(mla_kv_norm_g),
               vec(ret_gn_g), lg_lanes[:, 0:1], lg_lanes[:, 1:2],
               jnp.pad(conv_w, ((0, 0), (0, 1), (0, 0))), vec(conv_b), vec(conv_ln_g), vec(conv_ln_b))

    xc = ctx
    for l in range(DEPTH):
        last = l == DEPTH - 1
        (w_ext, wuq, wukv, wo, w1, w2, pre1, post1, pre2, post2, qng, kvg, gng, lgf, lgb,
         cw, cb, clg, clb) = [_Layered(a, l) for a in stacks]
        lat = _ModRows(mod_table, l * MOD_ROWS * 6, 6)
        cx = _ModRows(mod_table, (l * MOD_ROWS + B) * 6, 0)
        lgd = ret_log_decay[l]

        proj = lambda xx, md, tabs: _inproj(xx, md, pre1, w_ext, qng, wuq, kvg, wukv, tabs)
        aL, rqL, rkL, rvL, rgL, mqL, mkL, mveL, mvoL = proj(x, lat, tabs_lat)
        aC, rqC, rkC, rvC, rgC, mqC, mkC, mveC, mvoC = proj(xc, cx, tabs_ctx)

        yrC, sf, sb = _retention(lgd, rqC, rkC, rvC, rgC, gng, lgf, lgb, zstate, zstate)
        yrL, _, _ = _retention(lgd, rqL, rkL, rvL, rgL, gng, lgf, lgb, sf, sb)

        attL = _attention(mqL, [(mkC, mveC, mvoC), (mkL, mveL, mvoL)])
        mix = lambda a, yr, att, md, xx: _mixer(a, cw, cb, clg, clb, yr, att, wo, post1, md, xx)
        ffn = lambda xx, md: _ffn(xx, md, pre2, w1, w2, post2)
        x = mix(aL, yrL, attL, lat, x)
        if not last:
            attC = _attention(mqC, [(mkC, mveC, mvoC)])
            xc = mix(aC, yrC, attC, cx, xc)
        x = ffn(x, lat)
        if not last:
            xc = ffn(xc, cx)
    return x
```

```python
import functools

import numpy as np
import jax
import jax.numpy as jnp
from jax import lax
from jax.experimental import pallas as pl
from jax.experimental.pallas import tpu as pltpu

F32 = jnp.float32
BF16 = jnp.bfloat16

D_MODEL = 1024
DEPTH = 2
GRID_W = 64
EPS = 1e-6
LOG2E = 1.4426950408889634
ROPE_BASE = 10000.0
CONV_CH = 256
CONV_K = 31
RET_HEADS = 4
RET_DK = 64
RET_DV = 64
MLA_HEADS = 8
MLA_Q_RANK = 256
MLA_KV_RANK = 128
MLA_NOPE = 64
MLA_ROPE = 32
MLA_V = 64
D_FF = 2816

LANES = 128
SUBLANES = 8
HALO = 16
CONV_SPAN = 24
ROW_TILE = 512
FFN_TILE = 512
RET_CHUNK = 256
ATT_TQ = 512
ATT_TK = 512
ATT_AHEAD = 2
ATT_SLOTS = ATT_AHEAD + 1
ATT_ROWS = 32
MXU_DIM = 256
FF_CHUNKS = ((0, 6 * MXU_DIM), (6 * MXU_DIM, D_FF))
VMEM_LIMIT = 56 * 1024 * 1024

_OFF_A = 0
_OFF_Q = 512
_OFF_K = 768
_OFF_V = 1024
_OFF_G = 1280
_OFF_CQ = 1536
_OFF_CKV = 1792
_OFF_KRP = 1920
D_EXT = 2048
MOD_ROWS = 8
MQ_W = MLA_HEADS * LANES


def _sigmoid(x):
    return 1.0 / (1.0 + jnp.exp(-x))


def _silu(x):
    return x * _sigmoid(x)


def _params(*sem):
    return pltpu.CompilerParams(dimension_semantics=sem, vmem_limit_bytes=VMEM_LIMIT)


class _Layered:
    def __init__(self, arr, layer):
        self.arr, self.layer = arr, layer

    @property
    def spec(self):
        n, layer = self.arr.ndim - 1, self.layer
        return pl.BlockSpec((None,) + self.arr.shape[1:], lambda *_: (layer,) + (0,) * n,
                            pipeline_mode=pl.Buffered(1))


class _ModRows:
    def __init__(self, table, base, per_batch):
        self.table, self.base, self.per_batch = table, base, per_batch

    def spec(self, j):
        base, step = self.base + j, self.per_batch
        return pl.BlockSpec((1, 1, self.table.shape[-1]), lambda b, i: (base + step * b, 0, 0))


def _mod_kernel(cv_ref, w_ref, b_ref, o_ref):
    s = _silu(cv_ref[...])
    o_ref[0] = jnp.dot(s.astype(BF16), w_ref[0].astype(BF16), preferred_element_type=F32) + b_ref[0]


def _modulation(cv, mod_w, mod_b):
    L, D, N = mod_w.shape
    tn = 1536
    return pl.pallas_call(
        _mod_kernel,
        out_shape=jax.ShapeDtypeStruct((L, 8, N), F32),
        grid=(L, N // tn),
        in_specs=[pl.BlockSpec((8, D), lambda l, j: (0, 0)),
                  pl.BlockSpec((1, D, tn), lambda l, j: (l, 0, j)),
                  pl.BlockSpec((1, 1, tn), lambda l, j: (l, 0, j))],
        out_specs=pl.BlockSpec((1, 8, tn), lambda l, j: (l, 0, j)),
        compiler_params=_params("arbitrary", "arbitrary"),
        name="modulation",
    )(cv, mod_w, mod_b.reshape(L, 1, N))


def _rope(x, cos, sin_up, sin_dn, f):
    return x * cos + pltpu.roll(x, LANES - f, 1) * sin_up + pltpu.roll(x, f, 1) * sin_dn


def _inproj_kernel(x_ref, sh_ref, sc_ref, g_ref, w_ref, qng_ref, wuq_ref, kvg_ref, wukv_ref, tab_ref,
                   a_ref, rq_ref, rk_ref, rv_ref, rg_ref, mq_ref, mk_ref, mve_ref, mvo_ref):
    x = x_ref[0]
    h = x * lax.rsqrt(jnp.mean(x * x, axis=-1, keepdims=True) + EPS) * g_ref[...]
    hb = (h * (1.0 + sc_ref[0]) + sh_ref[0]).astype(BF16)

    def proj(lo, hi):
        return jnp.dot(hb, w_ref[:, lo:hi], preferred_element_type=F32)

    pm = proj(_OFF_CQ, D_EXT)
    pr = proj(_OFF_Q, _OFF_V)
    cq = pm[:, 0:MLA_Q_RANK]
    qn = cq * lax.rsqrt(jnp.mean(cq * cq, axis=-1, keepdims=True) + EPS) * qng_ref[...]
    qq = jnp.dot(qn.astype(BF16), wuq_ref[...], preferred_element_type=F32)
    ckv = pm[:, _OFF_CKV - _OFF_CQ:_OFF_KRP - _OFF_CQ]
    kvn = ckv * lax.rsqrt(jnp.mean(ckv * ckv, axis=-1, keepdims=True) + EPS) * kvg_ref[...]
    kv = jnp.dot(kvn.astype(BF16), wukv_ref[...], preferred_element_type=F32)
    pa = proj(_OFF_A, _OFF_Q)
    pv = proj(_OFF_V, _OFF_CQ)

    a_ref[0] = pa[:, 0:CONV_CH] * _sigmoid(pa[:, CONV_CH:2 * CONV_CH])

    ret_tabs = (tab_ref[0], tab_ref[1], tab_ref[2], RET_DK // 4)
    for i in range(2):
        lo, hi = i * LANES, (i + 1) * LANES
        rq_ref[0, :, lo:hi] = _rope(pr[:, lo:hi], *ret_tabs).astype(BF16)
        rk_ref[0, :, lo:hi] = _rope(pr[:, 256 + lo:256 + hi], *ret_tabs).astype(BF16)
    rv_ref[0] = pv[:, 0:256].astype(BF16)
    rg_ref[0] = pv[:, 256:512]

    mla_tabs = (tab_ref[3], tab_ref[4], tab_ref[5], MLA_ROPE // 4)
    scale = float((MLA_NOPE + MLA_ROPE) ** -0.5 * LOG2E)
    kr_slot = _rope(pm[:, _OFF_KRP - _OFF_CQ:D_EXT - _OFF_CQ], *mla_tabs)
    for hh in range(MLA_HEADS):
        lo, hi = hh * LANES, (hh + 1) * LANES
        mq_ref[0, :, lo:hi] = (_rope(qq[:, lo:hi], *mla_tabs) * scale).astype(BF16)
        mk_ref[0, :, lo:hi] = (kv[:, lo:hi] + kr_slot).astype(BF16)
    v = kv[:, MQ_W:]
    lane = lax.broadcasted_iota(jnp.int32, v.shape, 1)
    lane = lane % LANES
    even = lane < MLA_V
    mve_ref[0] = jnp.where(even, v, jnp.where(lane == MLA_V, 1.0, 0.0)).astype(BF16)
    mvo_ref[0] = jnp.where(even, jnp.where(lane == 0, 1.0, 0.0), v).astype(BF16)


def _inproj(x, mod, g, w_ext, qng, wuq, kvg, wukv, tabs):
    B, L, D = x.shape
    tm = min(ROW_TILE, L)
    row = lambda w: pl.BlockSpec((1, tm, w), lambda b, i: (b, i, 0))
    outs = [(CONV_CH, F32), (256, BF16), (256, BF16), (256, BF16), (256, F32),
            (MQ_W, BF16), (MQ_W, BF16), (512, BF16), (512, BF16)]
    layered = [g, w_ext, qng, wuq, kvg, wukv]
    return pl.pallas_call(
        _inproj_kernel,
        out_shape=[jax.ShapeDtypeStruct((B, L, w), dt) for w, dt in outs],
        grid=(B, L // tm),
        in_specs=[row(D), mod.spec(0), mod.spec(1)] + [a.spec for a in layered]
        + [pl.BlockSpec((tabs.shape[0], tm, LANES), lambda b, i: (0, i, 0))],
        out_specs=[row(w) for w, _ in outs],
        compiler_params=_params("arbitrary", "arbitrary"),
        name="inproj",
    )(x, mod.table, mod.table, *[a.arr for a in layered], tabs)


def _ret_kernel(lgd_ref, q_ref, k_ref, v_ref, g_ref, gng_ref, lgf_ref, lgb_ref, sf0_ref, sb0_ref,
                y_ref, sfo_ref, sbo_ref, w_scr, dec_scr, kvf_scr, kvb_scr, sfs_scr, sbs_scr, *, n_chunks):
    C = RET_CHUNK
    N = n_chunks
    P = RET_HEADS // 2
    U = 2 if N % 2 == 0 else 1

    @pl.when(pl.program_id(0) == 0)
    def _():
        t = lax.broadcasted_iota(jnp.int32, (C, C), 0)
        m = lax.broadcasted_iota(jnp.int32, (C, C), 1)
        d = (t - m).astype(F32)
        for hh in range(RET_HEADS):
            w_scr[hh] = jnp.exp(jnp.where(d >= 0.0, lgd_ref[0, hh] * d, lgd_ref[1, hh] * (-d)))
        j = lax.broadcasted_iota(jnp.int32, (C, LANES), 0).astype(F32)
        for pi in range(P):
            lgf = lgf_ref[:, pi * LANES:(pi + 1) * LANES]
            lgb = lgb_ref[:, pi * LANES:(pi + 1) * LANES]
            dec_scr[pi, 0] = jnp.exp(lgf * (C - 1.0 - j))
            dec_scr[pi, 1] = jnp.exp(lgb * j)
            dec_scr[pi, 2] = jnp.exp(lgf * (j + 1.0))
            dec_scr[pi, 3] = jnp.exp(lgb * (C - j))

    first = lax.broadcasted_iota(jnp.int32, (C, LANES), 1) < RET_DV
    r128 = lax.broadcasted_iota(jnp.int32, (LANES, LANES), 0)
    c128 = lax.broadcasted_iota(jnp.int32, (LANES, LANES), 1)
    blockdiag = (r128 < RET_DK) == (c128 < RET_DV)
    dn = (((1,), (1,)), ((), ()))

    def sums_one(i, pi):
        rows = pl.ds(pl.multiple_of(i * C, C), C)
        lo, hi = pi * LANES, (pi + 1) * LANES
        kf = k_ref[0, rows, lo:hi].astype(F32)
        vv = v_ref[0, rows, lo:hi]
        kzf = (kf * dec_scr[pi, 0]).T.astype(BF16)
        kzb = (kf * dec_scr[pi, 1]).T.astype(BF16)
        kvf_scr[pi, i] = jnp.where(blockdiag, jnp.dot(kzf, vv, preferred_element_type=F32), 0.0)
        kvb_scr[pi, i] = jnp.where(blockdiag, jnp.dot(kzb, vv, preferred_element_type=F32), 0.0)

    def out_one(i, pi):
        rows = pl.ds(pl.multiple_of(i * C, C), C)
        lo, hi = pi * LANES, (pi + 1) * LANES
        q = q_ref[0, rows, lo:hi]
        k = k_ref[0, rows, lo:hi]
        vv = v_ref[0, rows, lo:hi]
        zero = jnp.zeros_like(q)
        s0 = lax.dot_general(jnp.where(first, q, zero), k, dn, preferred_element_type=F32)
        s1 = lax.dot_general(jnp.where(first, zero, q), k, dn, preferred_element_type=F32)
        p0 = (s0 * w_scr[2 * pi]).astype(BF16)
        p1 = (s1 * w_scr[2 * pi + 1]).astype(BF16)
        o = jnp.dot(p0, jnp.where(first, vv, zero), preferred_element_type=F32)
        o = o + jnp.dot(p1, jnp.where(first, zero, vv), preferred_element_type=F32)
        o = o + jnp.dot(q, sfs_scr[pi, i].astype(BF16), preferred_element_type=F32) * dec_scr[pi, 2]
        o = o + jnp.dot(q, sbs_scr[pi, i].astype(BF16), preferred_element_type=F32) * dec_scr[pi, 3]
        inv = 1.0 / RET_DV
        tot = jnp.sum(o, axis=-1, keepdims=True)
        m0 = jnp.sum(jnp.where(first, o, 0.0), axis=-1, keepdims=True)
        mu = jnp.where(first, m0, tot - m0) * inv
        dlt = o - mu
        d2 = dlt * dlt
        tot2 = jnp.sum(d2, axis=-1, keepdims=True)
        v0 = jnp.sum(jnp.where(first, d2, 0.0), axis=-1, keepdims=True)
        var = jnp.where(first, v0, tot2 - v0) * inv
        y = dlt * lax.rsqrt(var + EPS) * gng_ref[:, lo:hi]
        y_ref[0, rows, lo:hi] = (y * _silu(g_ref[0, rows, lo:hi])).astype(BF16)

    def over_chunks(fn):
        def body(t, carry):
            for u in range(U):
                for pi in range(P):
                    fn(t * U + u, pi)
            return carry
        lax.fori_loop(0, N // U, body, 0)

    over_chunks(sums_one)

    gc_f = [jnp.exp(lgf_ref[:, pi * LANES:(pi + 1) * LANES] * float(C)) for pi in range(P)]
    gc_b = [jnp.exp(lgb_ref[:, pi * LANES:(pi + 1) * LANES] * float(C)) for pi in range(P)]

    def fwd_scan(i, ss):
        for pi in range(P):
            sfs_scr[pi, i] = ss[pi]
        return tuple(ss[pi] * gc_f[pi] + kvf_scr[pi, i] for pi in range(P))

    def bwd_scan(ii, ss):
        i = N - 1 - ii
        for pi in range(P):
            sbs_scr[pi, i] = ss[pi]
        return tuple(ss[pi] * gc_b[pi] + kvb_scr[pi, i] for pi in range(P))

    sf = lax.fori_loop(0, N, fwd_scan, tuple(sf0_ref[0, pi] for pi in range(P)))
    sb = lax.fori_loop(0, N, bwd_scan, tuple(sb0_ref[0, pi] for pi in range(P)))
    for pi in range(P):
        sfo_ref[0, pi] = sf[pi]
        sbo_ref[0, pi] = sb[pi]

    over_chunks(out_one)


def _retention(lgd, q, k, v, g, gng, lgf, lgb, sf0, sb0):
    B, L, W = q.shape
    C = RET_CHUNK
    N = L // C
    P = RET_HEADS // 2
    seq = lambda: pl.BlockSpec((1, L, W), lambda b: (b, 0, 0))
    st = lambda: pl.BlockSpec((1, P, LANES, LANES), lambda b: (b, 0, 0, 0))
    st_shape = jax.ShapeDtypeStruct((B, P, LANES, LANES), F32)
    return pl.pallas_call(
        functools.partial(_ret_kernel, n_chunks=N),
        out_shape=[jax.ShapeDtypeStruct((B, L, W), BF16), st_shape, st_shape],
        grid=(B,),
        in_specs=[pl.BlockSpec(memory_space=pltpu.SMEM), seq(), seq(), seq(), seq(),
                  gng.spec, lgf.spec, lgb.spec, st(), st()],
        out_specs=[seq(), st(), st()],
        scratch_shapes=[pltpu.VMEM((RET_HEADS, C, C), F32), pltpu.VMEM((P, 4, C, LANES), F32)]
        + [pltpu.VMEM((P, N, LANES, LANES), F32) for _ in range(4)],
        compiler_params=_params("arbitrary"),
        name="retention",
    )(lgd, q, k, v, g, gng.arr, lgf.arr, lgb.arr, sf0, sb0)


def _attn_kernel(*refs, lengths):
    q_ref = refs[0]
    src = refs[1:1 + 3 * len(lengths)]
    o_ref, s_scr, p_scr, m_scr, a_scr, acc_scr = refs[1 + 3 * len(lengths):]
    tq = q_ref.shape[1]
    dn = (((1,), (1,)), ((), ()))
    H = MLA_HEADS
    RB = ATT_ROWS

    m_scr[...] = jnp.full(m_scr.shape, -1e30, F32)
    acc_scr[...] = jnp.zeros(acc_scr.shape, F32)

    def step(k_ref, ve_ref, vo_ref, rows, tk):
        def scores(h):
            q = q_ref[0, :, h * LANES:(h + 1) * LANES]
            k = k_ref[0, rows, h * LANES:(h + 1) * LANES]
            s_scr[h % ATT_SLOTS, :, 0:tk] = lax.dot_general(q, k, dn, preferred_element_type=F32)

        for h in range(ATT_AHEAD):
            scores(h)
        for h in range(H):
            if h + ATT_AHEAD < H:
                scores(h + ATT_AHEAD)
            for r0 in range(0, tq, RB):
                s = s_scr[h % ATT_SLOTS, r0:r0 + RB, 0:tk]
                m_old = m_scr[h, r0:r0 + RB, :]
                n = jnp.maximum(m_old, jnp.max(s, axis=-1, keepdims=True))
                p_scr[h % 2, r0:r0 + RB, 0:tk] = jnp.exp2(s - jnp.tile(n, (1, tk // LANES))).astype(BF16)
                a_scr[h % 2, r0:r0 + RB, :] = jnp.exp2(m_old - n)
                m_scr[h, r0:r0 + RB, :] = n
            v_ref = ve_ref if h % 2 == 0 else vo_ref
            v = v_ref[0, rows, (h // 2) * LANES:(h // 2 + 1) * LANES]
            acc_scr[h] = acc_scr[h] * a_scr[h % 2] + jnp.dot(p_scr[h % 2, :, 0:tk], v, preferred_element_type=F32)

    for si, length in enumerate(lengths):
        k_ref, ve_ref, vo_ref = src[3 * si:3 * si + 3]
        tk = min(ATT_TK, length)
        n = length // tk
        if n == 1:
            step(k_ref, ve_ref, vo_ref, pl.ds(0, tk), tk)
        else:
            @pl.loop(0, n)
            def _(c):
                step(k_ref, ve_ref, vo_ref, pl.ds(pl.multiple_of(c * tk, tk), tk), tk)

    first = lax.broadcasted_iota(jnp.int32, (tq, LANES), 1) < MLA_V
    for pj in range(H // 2):
        a0, a1 = acc_scr[2 * pj], acc_scr[2 * pj + 1]
        out = jnp.where(first, a0 / a0[:, MLA_V:MLA_V + 1], a1 / a1[:, 0:1])
        o_ref[0, :, pj * LANES:(pj + 1) * LANES] = out.astype(BF16)


def _attention(q, sources):
    B, L, _ = q.shape
    tq = min(ATT_TQ, L)
    lengths =tuple(s[0].shape[1] for s in sources)
    in_specs = [pl.BlockSpec((1, tq, MQ_W), lambda b, i: (b, i, 0))]
    args = [q]
    for k, ve, vo in sources:
        S = k.shape[1]
        in_specs += [pl.BlockSpec((1, S, MQ_W), lambda b, i: (b, 0, 0)),
                     pl.BlockSpec((1, S, 512), lambda b, i: (b, 0, 0)),
                     pl.BlockSpec((1, S, 512), lambda b, i: (b, 0, 0))]
        args += [k, ve, vo]
    return pl.pallas_call(
        functools.partial(_attn_kernel, lengths=lengths),
        out_shape=jax.ShapeDtypeStruct((B, L, MLA_HEADS * MLA_V), BF16),
        grid=(B, L // tq),
        in_specs=in_specs,
        out_specs=pl.BlockSpec((1, tq, MLA_HEADS * MLA_V), lambda b, i: (b, i, 0)),
        scratch_shapes=[pltpu.VMEM((ATT_SLOTS, tq, ATT_TK), F32),
                        pltpu.VMEM((2, tq, ATT_TK), BF16),
                        pltpu.VMEM((MLA_HEADS, tq, LANES), F32),
                        pltpu.VMEM((2, tq, LANES), F32),
                        pltpu.VMEM((MLA_HEADS, tq, LANES), F32)],
        compiler_params=_params("arbitrary", "arbitrary"),
        name="attention",
    )(*args)


def _mixer_kernel(a_ref, ap_ref, an_ref, cw_ref, cb_ref, lg_ref, lb_ref, yr_ref, at_ref, wo_ref,
                  pg_ref, gate_ref, x_ref, o_ref, win_scr, shf_scr, yc_scr):
    tm = a_ref.shape[1]
    i = pl.program_id(1)
    n = pl.num_programs(1)
    win_scr[0:HALO] = jnp.where(i > 0, ap_ref[0], 0.0)
    win_scr[HALO:HALO + tm] = a_ref[0]
    win_scr[HALO + tm:HALO + tm + HALO] = jnp.where(i < n - 1, an_ref[0], 0.0)
    span = tm + CONV_SPAN
    for ph in range(1, SUBLANES):
        shf_scr[ph - 1] = win_scr[ph:ph + span, :]
    sub = 64
    off = HALO - CONV_K // 2
    for r0 in range(0, tm, sub):
        acc = jnp.zeros((sub, CONV_CH), F32) + cb_ref[...]
        for t in range(CONV_K):
            ph = (off + t) % SUBLANES
            base = r0 + off + t - ph
            rows = win_scr[base:base + sub, :] if ph == 0 else shf_scr[ph - 1, base:base + sub, :]
            acc = acc + cw_ref[t:t + 1, :] * rows
        mu = jnp.mean(acc, axis=-1, keepdims=True)
        d = acc - mu
        var = jnp.mean(d * d, axis=-1, keepdims=True)
        yc = d * lax.rsqrt(var + EPS) * lg_ref[...] + lb_ref[...]
        yc_scr[r0:r0 + sub, :] = _silu(yc).astype(BF16)
    y = jnp.dot(yc_scr[...], wo_ref[0:256, :], preferred_element_type=F32)
    y = y + jnp.dot(yr_ref[0], wo_ref[256:512, :], preferred_element_type=F32)
    y = y + jnp.dot(at_ref[0], wo_ref[512:1024, :], preferred_element_type=F32)
    r = y * lax.rsqrt(jnp.mean(y * y, axis=-1, keepdims=True) + EPS) * pg_ref[...]
    o_ref[0] = x_ref[0] + gate_ref[0] * r


def _mixer(a, cw, cb, lg, lb, yr, att, wo, pg, mod, x):
    B, L, D = x.shape
    tm = min(ROW_TILE, L)
    hb = tm // HALO
    nh = L // HALO
    row = lambda w: pl.BlockSpec((1, tm, w), lambda b, i: (b, i, 0))
    return pl.pallas_call(
        _mixer_kernel,
        out_shape=jax.ShapeDtypeStruct((B, L, D), F32),
        grid=(B, L // tm),
        in_specs=[row(CONV_CH),
                  pl.BlockSpec((1, HALO, CONV_CH), lambda b, i: (b, jnp.maximum(i * hb - 1, 0), 0)),
                  pl.BlockSpec((1, HALO, CONV_CH), lambda b, i: (b, jnp.minimum((i + 1) * hb, nh - 1), 0)),
                  cw.spec, cb.spec, lg.spec, lb.spec, row(256), row(512), wo.spec, pg.spec, mod.spec(2), row(D)],
        out_specs=row(D),
        scratch_shapes=[pltpu.VMEM((tm + 2 * HALO, CONV_CH), F32),
                        pltpu.VMEM((SUBLANES - 1, tm + CONV_SPAN, CONV_CH), F32),
                        pltpu.VMEM((tm, CONV_CH), BF16)],
        compiler_params=_params("arbitrary", "arbitrary"),
        name="mixer_out",
    )(a, a, a, cw.arr, cb.arr, lg.arr, lb.arr, yr, att, wo.arr, pg.arr, mod.table, x)


def _ffn_kernel(x_ref, sh_ref, sc_ref, g_ref, w1_ref, w2_ref, pg_ref, gate_ref, o_ref):
    x = x_ref[0]
    h = x * lax.rsqrt(jnp.mean(x * x, axis=-1, keepdims=True) + EPS) * g_ref[...]
    h = (h * (1.0 + sc_ref[0]) + sh_ref[0]).astype(BF16)
    y = jnp.zeros(x.shape, F32)
    for lo, hi in FF_CHUNKS:
        u = jnp.dot(h, w1_ref[:, lo:hi], preferred_element_type=F32)
        gt = jnp.dot(h, w1_ref[:, D_FF + lo:D_FF + hi], preferred_element_type=F32)
        y = y + jnp.dot((_silu(gt) * u).astype(BF16), w2_ref[lo:hi, :], preferred_element_type=F32)
    r = y * lax.rsqrt(jnp.mean(y * y, axis=-1, keepdims=True) + EPS) * pg_ref[...]
    o_ref[0] = x + gate_ref[0] * r


def _ffn(x, mod, g, w1, w2, pg):
    B, L, D = x.shape
    tm = min(FFN_TILE, L)
    row = pl.BlockSpec((1, tm, D), lambda b, i: (b, i, 0))
    return pl.pallas_call(
        _ffn_kernel,
        out_shape=jax.ShapeDtypeStruct((B, L, D), F32),
        grid=(B, L // tm),
        in_specs=[row, mod.spec(3), mod.spec(4), g.spec, w1.spec, w2.spec, pg.spec, mod.spec(5)],
        out_specs=row,
        compiler_params=_params("arbitrary", "arbitrary"),
        name="ffn",
    )(x, mod.table, mod.table, g.arr, w1.arr, w2.arr, pg.arr, mod.table)


def _rope_tables(T, rotary):
    t = np.arange(T)
    rowp = (t // GRID_W).astype(np.float64)[:, None]
    colp = (t % GRID_W).astype(np.float64)[:, None]

    def half(f):
        inv = ROPE_BASE ** (-np.arange(f, dtype=np.float64) / f)
        ar, ac = rowp * inv[None, :], colp * inv[None, :]
        cos = np.concatenate([np.cos(ar)] * 2 + [np.cos(ac)] * 2, axis=-1)
        sin_r, sin_c, z = np.sin(ar), np.sin(ac), np.zeros_like(ar)
        sin_up = np.concatenate([-sin_r, z, -sin_c, z], axis=-1)
        sin_dn = np.concatenate([z, sin_r, z, sin_c], axis=-1)
        if not rotary:
            cos, sin_up, sin_dn = np.ones_like(cos), np.zeros_like(cos), np.zeros_like(cos)
        return cos, sin_up, sin_dn

    ret = [np.tile(a, (1, 2)) for a in half(RET_DK // 4)]

    def pad(a, fill):
        return np.concatenate([np.full((T, MLA_NOPE), fill), a,
                               np.full((T, LANES - MLA_NOPE - MLA_ROPE), fill)], axis=-1)

    cm, su, sd = half(MLA_ROPE // 4)
    return jnp.asarray(np.stack(ret + [pad(cm, 1.0), pad(su, 0.0), pad(sd, 0.0)]).astype(np.float32))


def _stacked_weights(w_in, mla_w_uq, mla_w_ukv, w_out, ffn_w_in, ffn_w_out):
    L, D, _ = w_in.shape
    offs = [0]
    for s in (2 * CONV_CH, 256, 256, 256, 256, MLA_Q_RANK, MLA_KV_RANK, MLA_ROPE):
        offs.append(offs[-1] + s)
    wa, wq, wk, wv, wg, wcq, wckv, wkr = [w_in[:, :, offs[i]:offs[i + 1]] for i in range(8)]
    zl = jnp.zeros((L, D, MLA_NOPE), F32)
    zr = jnp.zeros((L, D, LANES - MLA_NOPE - MLA_ROPE), F32)
    w_ext = jnp.concatenate([wa, wq, wk * (RET_DK ** -0.5), wv, wg, wcq, wckv, zl, wkr, zr], axis=2).astype(BF16)

    uq = mla_w_uq.reshape(L, MLA_Q_RANK, MLA_HEADS, MLA_NOPE + MLA_ROPE)
    zq = jnp.zeros((L, MLA_Q_RANK, MLA_HEADS, LANES - MLA_NOPE - MLA_ROPE), F32)
    wuq = jnp.concatenate([uq, zq], axis=-1).reshape(L, MLA_Q_RANK, MQ_W).astype(BF16)

    ukv = mla_w_ukv.reshape(L, MLA_KV_RANK, MLA_HEADS, MLA_NOPE + MLA_V)
    zk = jnp.zeros((L, MLA_KV_RANK, MLA_HEADS, LANES - MLA_NOPE), F32)
    uk_pad = jnp.concatenate([ukv[..., :MLA_NOPE], zk], axis=-1).reshape(L, MLA_KV_RANK, MQ_W)
    uv = ukv[..., MLA_NOPE:].reshape(L, MLA_KV_RANK, MLA_HEADS * MLA_V)
    wukv = jnp.concatenate([uk_pad, uv], axis=2).astype(BF16)
    return w_ext, wuq, wukv, w_out.astype(BF16), ffn_w_in.astype(BF16), ffn_w_out.astype(BF16)


def kernel(x, c, ctx, c_ctx, mod_w, mod_b, pre1_g, post1_g, pre2_g, post2_g, w_in, conv_w, conv_b,
           conv_ln_g, conv_ln_b, ret_log_decay, ret_gn_g, mla_q_norm_g, mla_w_uq, mla_kv_norm_g,
           mla_w_ukv, w_out, ffn_w_in, ffn_w_out):
    B, T, D = x.shape
    TC = ctx.shape[1]
    cv = jnp.concatenate([c, c_ctx[None], jnp.zeros((MOD_ROWS - B - 1, D), F32)], axis=0)
    mod_table = _modulation(cv, mod_w, mod_b).reshape(DEPTH * MOD_ROWS * 6, 1, D)

    tabs_lat = _rope_tables(T, True)
    tabs_ctx = _rope_tables(TC, False)
    zstate = jnp.zeros((B, RET_HEADS // 2, LANES, LANES), F32)

    stacks = _stacked_weights(w_in, mla_w_uq, mla_w_ukv, w_out, ffn_w_in, ffn_w_out)
    vec = lambda a: a.reshape(DEPTH, 1, -1)
    lg_lanes = jnp.repeat(ret_log_decay, RET_DK, axis=-1)
    stacks += (vec(pre1_g), vec(post1_g), vec(pre2_g), vec(post2_g), vec(mla_q_norm_g), vec(mla_kv_norm_g),
               vec(ret_gn_g), lg_lanes[:, 0:1], lg_lanes[:, 1:2],
               jnp.pad(conv_w, ((0, 0), (0, 1), (0, 0))), vec(conv_b), vec(conv_ln_g), vec(conv_ln_b))

    xc = ctx
    for l in range(DEPTH):
        last = l == DEPTH - 1
        (w_ext, wuq, wukv, wo, w1, w2, pre1, post1, pre2, post2, qng, kvg, gng, lgf, lgb,
         cw, cb, clg, clb) = [_Layered(a, l) for a in stacks]
        lat = _ModRows(mod_table, l * MOD_ROWS * 6, 6)
        cx = _ModRows(mod_table, (l * MOD_ROWS + B) * 6, 0)
        lgd = ret_log_decay[l]

        proj = lambda xx, md, tabs: _inproj(xx, md, pre1, w_ext, qng, wuq, kvg, wukv, tabs)
        aL, rqL, rkL, rvL, rgL, mqL, mkL, mveL, mvoL = proj(x, lat, tabs_lat)
        aC, rqC, rkC, rvC, rgC, mqC, mkC, mveC, mvoC = proj(xc, cx, tabs_ctx)

        yrC, sf, sb = _retention(lgd, rqC, rkC, rvC, rgC, gng, lgf, lgb, zstate, zstate)
        yrL, _, _ = _retention(lgd, rqL, rkL, rvL, rgL, gng, lgf, lgb, sf, sb)

        attL = _attention(mqL, [(mkC, mveC, mvoC), (mkL, mveL, mvoL)])
        mix = lambda a, yr, att, md, xx: _mixer(a, cw, cb, clg, clb, yr, att, wo, post1, md, xx)
        ffn = lambda xx, md: _ffn(xx, md, pre2, w1, w2, post2)
        x = mix(aL, yrL, attL, lat, x)
        if not last:
            attC = _attention(mqC, [(mkC, mveC, mvoC)])
            xc = mix(aC, yrC, attC, cx, xc)
        x = ffn(x, lat)
        if not last:
            xc = ffn(xc, cx)
    return x
```

```python
import functools

import numpy as np
import jax
import jax.numpy as jnp
from jax import lax
from jax.experimental import pallas as pl
from jax.experimental.pallas import tpu as pltpu

F32 = jnp.float32
BF16 = jnp.bfloat16

D_MODEL = 1024
DEPTH = 2
GRID_W = 64
EPS = 1e-6
LOG2E = 1.4426950408889634
ROPE_BASE = 10000.0
CONV_CH = 256
CONV_K = 31
RET_HEADS = 4
RET_DK = 64
RET_DV = 64
MLA_HEADS = 8
MLA_Q_RANK = 256
MLA_KV_RANK = 128
MLA_NOPE = 64
MLA_ROPE = 32
MLA_V = 64
D_FF = 2816

LANES = 128
SUBLANES = 8
HALO = 16
CONV_SPAN = 24
ROW_TILE = 512
FFN_TILE = 512
RET_CHUNK = 256
ATT_TQ = 512
ATT_TK = 512
ATT_AHEAD = 2
ATT_SLOTS = ATT_AHEAD + 1
ATT_ROWS = 32
MXU_DIM = 256
FF_CHUNKS = ((0, 6 * MXU_DIM), (6 * MXU_DIM, D_FF))
VMEM_LIMIT = 56 * 1024 * 1024

_OFF_A = 0
_OFF_Q = 512
_OFF_K = 768
_OFF_V = 1024
_OFF_G = 1280
_OFF_CQ = 1536
_OFF_CKV = 1792
_OFF_KRP = 1920
D_EXT = 2048
MOD_ROWS = 8
MQ_W = MLA_HEADS * LANES


def _sigmoid(x):
    return 1.0 / (1.0 + jnp.exp(-x))


def _silu(x):
    return x * _sigmoid(x)


def _params(*sem):
    return pltpu.CompilerParams(dimension_semantics=sem, vmem_limit_bytes=VMEM_LIMIT)


class _Layered:
    def __init__(self, arr, layer):
        self.arr, self.layer = arr, layer

    @property
    def spec(self):
        n, layer = self.arr.ndim - 1, self.layer
        return pl.BlockSpec((None,) + self.arr.shape[1:], lambda *_: (layer,) + (0,) * n,
                            pipeline_mode=pl.Buffered(1))


class _LayerLanes:
    def __init__(self, arr, layer, block, width):
        self.arr, self.layer, self.block, self.width = arr, layer, block, width

    @property
    def spec(self):
        layer, block = self.layer, self.block
        return pl.BlockSpec((None, 1, self.width), lambda *_: (layer, 0, block), pipeline_mode=pl.Buffered(1))


class _ModRows:
    def __init__(self, table, base, per_batch):
        self.table, self.base, self.per_batch = table, base, per_batch

    def spec(self, j):
        base, step = self.base + j, self.per_batch
        return pl.BlockSpec((1, 1, self.table.shape[-1]), lambda b, i: (base + step * b, 0, 0))


def _mod_kernel(cv_ref, w_ref, b_ref, o_ref):
    s = _silu(cv_ref[...])
    o_ref[0] = jnp.dot(s.astype(BF16), w_ref[0].astype(BF16), preferred_element_type=F32) + b_ref[0]


def _modulation(cv, mod_w, mod_b):
    L, D, N = mod_w.shape
    tn = 1536
    return pl.pallas_call(
        _mod_kernel,
        out_shape=jax.ShapeDtypeStruct((L, 8, N), F32),
        grid=(L, N // tn),
        in_specs=[pl.BlockSpec((8, D), lambda l, j: (0, 0)),
                  pl.BlockSpec((1, D, tn), lambda l, j: (l, 0, j)),
                  pl.BlockSpec((1, 1, tn), lambda l, j: (l, 0, j))],
        out_specs=pl.BlockSpec((1, 8, tn), lambda l, j: (l, 0, j)),
        compiler_params=_params("arbitrary", "arbitrary"),
        name="modulation",
    )(cv, mod_w, mod_b.reshape(L, 1, N))


def _rope(x, cos, sin_up, sin_dn, f):
    return x * cos + pltpu.roll(x, LANES - f, 1) * sin_up + pltpu.roll(x, f, 1) * sin_dn


def _inproj_kernel(x_ref, sh_ref, sc_ref, g_ref, w_ref, qng_ref, wuq_ref, kvg_ref, wukv_ref, tab_ref,
                   a_ref, rq_ref, rk_ref, rv_ref, rg_ref, mq_ref, mk_ref, mve_ref, mvo_ref, w_scr):
    @pl.when((pl.program_id(0) == 0) & (pl.program_id(1) == 0))
    def _():
        w_scr[:, _OFF_KRP:D_EXT] = jnp.zeros((D_MODEL, D_EXT - _OFF_KRP), BF16)
        for lo in range(0, _OFF_KRP, LANES):
            w_scr[:, lo:lo + LANES] = w_ref[:, lo:lo + LANES].astype(BF16)
        w_scr[:, _OFF_KRP:_OFF_KRP + MLA_ROPE] = w_ref[:, _OFF_KRP:_OFF_KRP + MLA_ROPE].astype(BF16)

    x = x_ref[0]
    h = x * lax.rsqrt(jnp.mean(x * x, axis=-1, keepdims=True) + EPS) * g_ref[...]
    hb = (h * (1.0 + sc_ref[0]) + sh_ref[0]).astype(BF16)

    def proj(lo, hi):
        return jnp.dot(hb, w_scr[:, lo:hi], preferred_element_type=F32)

    pm = proj(_OFF_CQ, D_EXT)
    pr = proj(_OFF_Q, _OFF_V)
    cq = pm[:, 0:MLA_Q_RANK]
    qn = cq * lax.rsqrt(jnp.mean(cq * cq, axis=-1, keepdims=True) + EPS) * qng_ref[...]
    qq = jnp.dot(qn.astype(BF16), wuq_ref[...], preferred_element_type=F32)
    ckv = pm[:, _OFF_CKV - _OFF_CQ:_OFF_KRP - _OFF_CQ]
    kvn = ckv * lax.rsqrt(jnp.mean(ckv * ckv, axis=-1, keepdims=True) + EPS) * kvg_ref[...]
    kv = jnp.dot(kvn.astype(BF16), wukv_ref[...], preferred_element_type=F32)
    pa = proj(_OFF_A, _OFF_Q)
    pv = proj(_OFF_V, _OFF_CQ)

    a_ref[0] = pa[:, 0:CONV_CH] * _sigmoid(pa[:, CONV_CH:2 * CONV_CH])

    ret_tabs = (tab_ref[0], tab_ref[1], tab_ref[2], RET_DK // 4)
    for i in range(2):
        lo, hi = i * LANES, (i + 1) * LANES
        rq_ref[0, :, lo:hi] = _rope(pr[:, lo:hi], *ret_tabs).astype(BF16)
        rk_ref[0, :, lo:hi] = (_rope(pr[:, 256 + lo:256 + hi], *ret_tabs) * float(RET_DK ** -0.5)).astype(BF16)
    rv_ref[0] = pv[:, 0:256].astype(BF16)
    rg_ref[0] = pv[:, 256:512]

    mla_tabs = (tab_ref[3], tab_ref[4], tab_ref[5], MLA_ROPE // 4)
    scale = float((MLA_NOPE + MLA_ROPE) ** -0.5 * LOG2E)
    kr_slot = _rope(pltpu.roll(pm[:, _OFF_KRP - _OFF_CQ:D_EXT - _OFF_CQ], MLA_NOPE, 1), *mla_tabs)
    for hh in range(MLA_HEADS):
        lo, hi = hh * LANES, (hh + 1) * LANES
        mq_ref[0, :, lo:hi] = (_rope(qq[:, lo:hi], *mla_tabs) * scale).astype(BF16)
        mk_ref[0, :, lo:hi] = (kv[:, lo:hi] + kr_slot).astype(BF16)
    v = kv[:, MQ_W:]
    lane = lax.broadcasted_iota(jnp.int32, v.shape, 1)
    lane = lane % LANES
    even = lane < MLA_V
    mve_ref[0] = jnp.where(even, v, jnp.where(lane == MLA_V, 1.0, 0.0)).astype(BF16)
    mvo_ref[0] = jnp.where(even, jnp.where(lane == 0, 1.0, 0.0), v).astype(BF16)


def _inproj(x, mod, g, w_ext, qng, wuq, kvg, wukv, tabs):
    B, L, D = x.shape
    tm = min(ROW_TILE, L)
    row = lambda w: pl.BlockSpec((1, tm, w), lambda b, i: (b, i, 0))
    outs = [(CONV_CH, F32), (256, BF16), (256, BF16), (256, BF16), (256, F32),
            (MQ_W, BF16), (MQ_W, BF16), (512, BF16), (512, BF16)]
    layered = [g, w_ext, qng, wuq, kvg, wukv]
    return pl.pallas_call(
        _inproj_kernel,
        out_shape=[jax.ShapeDtypeStruct((B, L, w), dt) for w, dt in outs],
        grid=(B, L // tm),
        in_specs=[row(D), mod.spec(0), mod.spec(1)] + [a.spec for a in layered]
        + [pl.BlockSpec((tabs.shape[0], tm, LANES), lambda b, i: (0, i, 0))],
        out_specs=[row(w) for w, _ in outs],
        scratch_shapes=[pltpu.VMEM((D, D_EXT), BF16)],
        compiler_params=_params("arbitrary", "arbitrary"),
        name="inproj",
    )(x, mod.table, mod.table, *[a.arr for a in layered], tabs)


def _ret_kernel(lgd_ref, q_ref, k_ref, v_ref, g_ref, gng_ref, lgf_ref, lgb_ref, sf0_ref, sb0_ref,
                y_ref, sfo_ref, sbo_ref, w_scr, dec_scr, kvf_scr, kvb_scr, sfs_scr, sbs_scr, *, n_chunks):
    C = RET_CHUNK
    N = n_chunks
    P = RET_HEADS // 2
    U = 4 if N % 4 == 0 else 1

    @pl.when(pl.program_id(0) == 0)
    def _():
        t = lax.broadcasted_iota(jnp.int32, (C, C), 0)
        m = lax.broadcasted_iota(jnp.int32, (C, C), 1)
        d = (t - m).astype(F32)
        for hh in range(RET_HEADS):
            w_scr[hh] = jnp.exp(jnp.where(d >= 0.0, lgd_ref[0, hh] * d, lgd_ref[1, hh] * (-d)))
        j = lax.broadcasted_iota(jnp.int32, (C, LANES), 0).astype(F32)
        for pi in range(P):
            lgf = lgf_ref[:, pi * LANES:(pi + 1) * LANES]
            lgb = lgb_ref[:, pi * LANES:(pi + 1) * LANES]
            dec_scr[pi, 0] = jnp.exp(lgf * (C - 1.0 - j))
            dec_scr[pi, 1] = jnp.exp(lgb * j)
            dec_scr[pi, 2] = jnp.exp(lgf * (j + 1.0))
            dec_scr[pi, 3] = jnp.exp(lgb * (C - j))

    first = lax.broadcasted_iota(jnp.int32, (C, LANES), 1) < RET_DV
    r128 = lax.broadcasted_iota(jnp.int32, (LANES, LANES), 0)
    c128 = lax.broadcasted_iota(jnp.int32, (LANES, LANES), 1)
    blockdiag = (r128 < RET_DK) == (c128 < RET_DV)
    dn = (((1,), (1,)), ((), ()))

    def sums_one(i, pi):
        rows = pl.ds(pl.multiple_of(i * C, C), C)
        lo, hi = pi * LANES, (pi + 1) * LANES
        kf = k_ref[0, rows, lo:hi].astype(F32)
        vv = v_ref[0, rows, lo:hi]
        kzf = (kf * dec_scr[pi, 0]).T.astype(BF16)
        kzb = (kf * dec_scr[pi, 1]).T.astype(BF16)
        kvf_scr[pi, i] = jnp.where(blockdiag, jnp.dot(kzf, vv, preferred_element_type=F32), 0.0)
        kvb_scr[pi, i] = jnp.where(blockdiag, jnp.dot(kzb, vv, preferred_element_type=F32), 0.0)

    def out_one(i, pi):
        rows = pl.ds(pl.multiple_of(i * C, C), C)
        lo, hi = pi * LANES, (pi + 1) * LANES
        q = q_ref[0, rows, lo:hi]
        k = k_ref[0, rows, lo:hi]
        vv = v_ref[0, rows, lo:hi]
        zero = jnp.zeros_like(q)
        s0 = lax.dot_general(jnp.where(first, q, zero), k, dn, preferred_element_type=F32)
        s1 = lax.dot_general(jnp.where(first, zero, q), k, dn, preferred_element_type=F32)
        p0 = (s0 * w_scr[2 * pi]).astype(BF16)
        p1 = (s1 * w_scr[2 * pi + 1]).astype(BF16)
        o = jnp.dot(p0, jnp.where(first, vv, zero), preferred_element_type=F32)
        o = o + jnp.dot(p1, jnp.where(first, zero, vv), preferred_element_type=F32)
        o = o + jnp.dot(q, sfs_scr[pi, i].astype(BF16), preferred_element_type=F32) * dec_scr[pi, 2]
        o = o + jnp.dot(q, sbs_scr[pi, i].astype(BF16), preferred_element_type=F32) * dec_scr[pi, 3]
        inv = 1.0 / RET_DV
        tot = jnp.sum(o, axis=-1, keepdims=True)
        m0 = jnp.sum(jnp.where(first, o, 0.0), axis=-1, keepdims=True)
        mu = jnp.where(first, m0, tot - m0) * inv
        dlt = o - mu
        d2 = dlt * dlt
        tot2 = jnp.sum(d2, axis=-1, keepdims=True)
        v0 = jnp.sum(jnp.where(first, d2, 0.0), axis=-1, keepdims=True)
        var = jnp.where(first, v0, tot2 - v0) * inv
        y = dlt * lax.rsqrt(var + EPS) * gng_ref[:, lo:hi]
        y_ref[0, rows, lo:hi] = (y * _silu(g_ref[0, rows, lo:hi])).astype(BF16)

    def over_chunks(fn):
        def body(t, carry):
            for u in range(U):
                for pi in range(P):
                    fn(t * U + u, pi)
            return carry
        lax.fori_loop(0, N // U, body, 0)

    over_chunks(sums_one)

    gc_f = [jnp.exp(lgf_ref[:, pi * LANES:(pi + 1) * LANES] * float(C)) for pi in range(P)]
    gc_b = [jnp.exp(lgb_ref[:, pi * LANES:(pi + 1) * LANES] * float(C)) for pi in range(P)]

    def fwd_scan(i, ss):
        for pi in range(P):
            sfs_scr[pi, i] = ss[pi]
        return tuple(ss[pi] * gc_f[pi] + kvf_scr[pi, i] for pi in range(P))

    def bwd_scan(ii, ss):
        i = N - 1 - ii
        for pi in range(P):
            sbs_scr[pi, i] = ss[pi]
        return tuple(ss[pi] * gc_b[pi] + kvb_scr[pi, i] for pi in range(P))

    sf = lax.fori_loop(0, N, fwd_scan, tuple(sf0_ref[0, pi] for pi in range(P)))
    sb = lax.fori_loop(0, N, bwd_scan, tuple(sb0_ref[0, pi] for pi in range(P)))
    for pi in range(P):
        sfo_ref[0, pi] = sf[pi]
        sbo_ref[0, pi] = sb[pi]

    over_chunks(out_one)


def _retention(lgd, q, k, v, g, gng, lgf, lgb, sf0, sb0):
    B, L, W = q.shape
    C = RET_CHUNK
    N = L // C
    P = RET_HEADS // 2
    seq = lambda: pl.BlockSpec((1, L, W), lambda b: (b, 0, 0))
    st = lambda: pl.BlockSpec((1, P, LANES, LANES), lambda b: (b, 0, 0, 0))
    st_shape = jax.ShapeDtypeStruct((B, P, LANES, LANES), F32)
    return pl.pallas_call(
        functools.partial(_ret_kernel, n_chunks=N),
        out_shape=[jax.ShapeDtypeStruct((B, L, W), BF16), st_shape, st_shape],
        grid=(B,),
        in_specs=[pl.BlockSpec(memory_space=pltpu.SMEM), seq(), seq(), seq(), seq(),
                  gng.spec, lgf.spec, lgb.spec, st(), st()],
        out_specs=[seq(), st(), st()],
        scratch_shapes=[pltpu.VMEM((RET_HEADS, C, C), F32), pltpu.VMEM((P, 4, C, LANES), F32)]
        + [pltpu.VMEM((P, N, LANES, LANES), F32) for _ in range(4)],
        compiler_params=_params("arbitrary"),
        name="retention",
    )(lgd, q, k, v, g, gng.arr, lgf.arr, lgb.arr, sf0, sb0)


def _attn_kernel(*refs, lengths):
    q_ref = refs[0]
    src = refs[1:1 + 3 * len(lengths)]
    o_ref, s_scr, p_scr, m_scr, a_scr, acc_scr = refs[1 + 3 * len(lengths):]
    tq = q_ref.shape[1]
    dn = (((1,), (1,)), ((), ()))
    H = MLA_HEADS
    RB = ATT_ROWS

    m_scr[...] = jnp.full(m_scr.shape, -1e30, F32)
    acc_scr[...] = jnp.zeros(acc_scr.shape, F32)

    def step(k_ref, ve_ref, vo_ref, rows, tk):
        def scores(h):
            q = q_ref[0, :, h * LANES:(h + 1) * LANES]
            k = k_ref[0, rows, h * LANES:(h + 1) * LANES]
            s_scr[h % ATT_SLOTS, :, 0:tk] = lax.dot_general(q, k, dn, preferred_element_type=F32)

        for h in range(ATT_AHEAD):
            scores(h)
        for h in range(H):
            if h + ATT_AHEAD < H:
                scores(h + ATT_AHEAD)
            for r0 in range(0, tq, RB):
                s = s_scr[h % ATT_SLOTS, r0:r0 + RB, 0:tk]
                m_old = m_scr[h, r0:r0 + RB, :]
                n = jnp.maximum(m_old, jnp.max(s, axis=-1, keepdims=True))
                p_scr[h % 2, r0:r0 + RB, 0:tk] = jnp.exp2(s - jnp.tile(n, (1, tk // LANES))).astype(BF16)
                a_scr[h % 2, r0:r0 + RB, :] = jnp.exp2(m_old - n)
                m_scr[h, r0:r0 + RB, :] = n
            v_ref = ve_ref if h % 2 == 0 else vo_ref
            v = v_ref[0, rows, (h // 2) * LANES:(h // 2 + 1) * LANES]
            acc_scr[h] = acc_scr[h] * a_scr[h % 2] + jnp.dot(p_scr[h % 2, :, 0:tk], v, preferred_element_type=F32)

    for si, length in enumerate(lengths):
        k_ref, ve_ref, vo_ref = src[3 * si:3 * si + 3]
        tk = min(ATT_TK, length)
        n = length // tk
        if n == 1:
            step(k_ref, ve_ref, vo_ref, pl.ds(0, tk), tk)
        else:
            @pl.loop(0, n)
            def _(c):
                step(k_ref, ve_ref, vo_ref, pl.ds(pl.multiple_of(c * tk, tk), tk), tk)

    first = lax.broadcasted_iota(jnp.int32, (tq, LANES), 1) < MLA_V
    for pj in range(H // 2):
        a0, a1 = acc_scr[2 * pj], acc_scr[2 * pj + 1]
        out = jnp.where(first, a0 / a0[:, MLA_V:MLA_V + 1], a1 / a1[:, 0:1])
        o_ref[0, :, pj * LANES:(pj + 1) * LANES] = out.astype(BF16)


def _attention(q, sources):
    B, L, _ = q.shape
    tq = min(ATT_TQ, L)
    lengths =tuple(s[0].shape[1] for s in sources)
    in_specs = [pl.BlockSpec((1, tq, MQ_W), lambda b, i: (b, i, 0))]
    args = [q]
    for k, ve, vo in sources:
        S = k.shape[1]
        in_specs += [pl.BlockSpec((1, S, MQ_W), lambda b, i: (b, 0, 0)),
                     pl.BlockSpec((1, S, 512), lambda b, i: (b, 0, 0)),
                     pl.BlockSpec((1, S, 512), lambda b, i: (b, 0, 0))]
        args += [k, ve, vo]
    return pl.pallas_call(
        functools.partial(_attn_kernel, lengths=lengths),
        out_shape=jax.ShapeDtypeStruct((B, L, MLA_HEADS * MLA_V), BF16),
        grid=(B, L // tq),
        in_specs=in_specs,
        out_specs=pl.BlockSpec((1, tq, MLA_HEADS * MLA_V), lambda b, i: (b, i, 0)),
        scratch_shapes=[pltpu.VMEM((ATT_SLOTS, tq, ATT_TK), F32),
                        pltpu.VMEM((2, tq, ATT_TK), BF16),
                        pltpu.VMEM((MLA_HEADS, tq, LANES), F32),
                        pltpu.VMEM((2, tq, LANES), F32),
                        pltpu.VMEM((MLA_HEADS, tq, LANES), F32)],
        compiler_params=_params("arbitrary", "arbitrary"),
        name="attention",
    )(*args)


def _mixer_kernel(a_ref, ap_ref, an_ref, cw_ref, cb_ref, lg_ref, lb_ref, yr_ref, at_ref, wo_ref,
                  pg_ref, gate_ref, x_ref, o_ref, win_scr, shf_scr, yc_scr, wo_scr):
    tm = a_ref.shape[1]
    i = pl.program_id(1)
    n = pl.num_programs(1)

    @pl.when((pl.program_id(0) == 0) & (i == 0))
    def _():
        for r0 in range(0, D_MODEL, 2 * LANES):
            wo_scr[r0:r0 + 2 * LANES, :] = wo_ref[r0:r0 + 2 * LANES, :].astype(BF16)

    win_scr[0:HALO] = jnp.where(i > 0, ap_ref[0], 0.0)
    win_scr[HALO:HALO + tm] = a_ref[0]
    win_scr[HALO + tm:HALO + tm + HALO] = jnp.where(i < n - 1, an_ref[0], 0.0)
    span = tm + CONV_SPAN
    for ph in range(1, SUBLANES):
        shf_scr[ph - 1] = win_scr[ph:ph + span, :]
    sub = 64
    off = HALO - CONV_K // 2
    for r0 in range(0, tm, sub):
        acc = jnp.zeros((sub, CONV_CH), F32) + cb_ref[...]
        for t in range(CONV_K):
            ph = (off + t) % SUBLANES
            base = r0 + off + t - ph
            rows = win_scr[base:base + sub, :] if ph == 0 else shf_scr[ph - 1, base:base + sub, :]
            acc = acc + cw_ref[t:t + 1, :] * rows
        mu = jnp.mean(acc, axis=-1, keepdims=True)
        d = acc - mu
        var = jnp.mean(d * d, axis=-1, keepdims=True)
        yc = d * lax.rsqrt(var + EPS) * lg_ref[...] + lb_ref[...]
        yc_scr[r0:r0 + sub, :] = _silu(yc).astype(BF16)
    y = jnp.dot(yc_scr[...], wo_scr[0:256, :], preferred_element_type=F32)
    y = y + jnp.dot(yr_ref[0], wo_scr[256:512, :], preferred_element_type=F32)
    y = y + jnp.dot(at_ref[0], wo_scr[512:1024, :], preferred_element_type=F32)
    r = y * lax.rsqrt(jnp.mean(y * y, axis=-1, keepdims=True) + EPS) * pg_ref[...]
    o_ref[0] = x_ref[0] + gate_ref[0] * r


def _mixer(a, cw, cb, lg, lb, yr, att, wo, pg, mod, x):
    B, L, D = x.shape
    tm = min(ROW_TILE, L)
    hb = tm // HALO
    nh = L // HALO
    row = lambda w: pl.BlockSpec((1, tm, w), lambda b, i: (b, i, 0))
    return pl.pallas_call(
        _mixer_kernel,
        out_shape=jax.ShapeDtypeStruct((B, L, D), F32),
        grid=(B, L // tm),
        in_specs=[row(CONV_CH),
                  pl.BlockSpec((1, HALO, CONV_CH), lambda b, i: (b, jnp.maximum(i * hb - 1, 0), 0)),
                  pl.BlockSpec((1, HALO, CONV_CH), lambda b, i: (b, jnp.minimum((i + 1) * hb, nh - 1), 0)),
                  cw.spec, cb.spec, lg.spec, lb.spec, row(256), row(512), wo.spec, pg.spec, mod.spec(2), row(D)],
        out_specs=row(D),
        scratch_shapes=[pltpu.VMEM((tm + 2 * HALO, CONV_CH), F32),
                        pltpu.VMEM((SUBLANES - 1, tm + CONV_SPAN, CONV_CH), F32),
                        pltpu.VMEM((tm, CONV_CH), BF16),
                        pltpu.VMEM((D, D), BF16)],
        compiler_params=_params("arbitrary", "arbitrary"),
        name="mixer_out",
    )(a, a, a, cw.arr, cb.arr, lg.arr, lb.arr, yr, att, wo.arr, pg.arr, mod.table, x)


def _ffn_kernel(x_ref, sh_ref, sc_ref, g_ref, w1_ref, w2_ref, pg_ref, gate_ref, o_ref):
    x = x_ref[0]
    h = x * lax.rsqrt(jnp.mean(x * x, axis=-1, keepdims=True) + EPS) * g_ref[...]
    h = (h * (1.0 + sc_ref[0]) + sh_ref[0]).astype(BF16)
    y = jnp.zeros(x.shape, F32)
    for lo, hi in FF_CHUNKS:
        u = jnp.dot(h, w1_ref[:, lo:hi], preferred_element_type=F32)
        gt = jnp.dot(h, w1_ref[:, D_FF + lo:D_FF + hi], preferred_element_type=F32)
        y = y + jnp.dot((_silu(gt) * u).astype(BF16), w2_ref[lo:hi, :], preferred_element_type=F32)
    r = y * lax.rsqrt(jnp.mean(y * y, axis=-1, keepdims=True) + EPS) * pg_ref[...]
    o_ref[0] = x + gate_ref[0] * r


def _ffn(x, mod, g, w1, w2, pg):
    B, L, D = x.shape
    tm = min(FFN_TILE, L)
    row = pl.BlockSpec((1, tm, D), lambda b, i: (b, i, 0))
    return pl.pallas_call(
        _ffn_kernel,
        out_shape=jax.ShapeDtypeStruct((B, L, D), F32),
        grid=(B, L // tm),
        in_specs=[row, mod.spec(3), mod.spec(4), g.spec, w1.spec, w2.spec, pg.spec, mod.spec(5)],
        out_specs=row,
        compiler_params=_params("arbitrary", "arbitrary"),
        name="ffn",
    )(x, mod.table, mod.table, g.arr, w1.arr, w2.arr, pg.arr, mod.table)


def _rope_tables(T, rotary):
    t = np.arange(T)
    rowp = (t // GRID_W).astype(np.float64)[:, None]
    colp = (t % GRID_W).astype(np.float64)[:, None]

    def half(f):
        inv = ROPE_BASE ** (-np.arange(f, dtype=np.float64) / f)
        ar, ac = rowp * inv[None, :], colp * inv[None, :]
        cos = np.concatenate([np.cos(ar)] * 2 + [np.cos(ac)] * 2, axis=-1)
        sin_r, sin_c, z = np.sin(ar), np.sin(ac), np.zeros_like(ar)
        sin_up = np.concatenate([-sin_r, z, -sin_c, z], axis=-1)
        sin_dn = np.concatenate([z, sin_r, z, sin_c], axis=-1)
        if not rotary:
            cos, sin_up, sin_dn = np.ones_like(cos), np.zeros_like(cos), np.zeros_like(cos)
        return cos, sin_up, sin_dn

    ret = [np.tile(a, (1, 2)) for a in half(RET_DK // 4)]

    def pad(a, fill):
        return np.concatenate([np.full((T, MLA_NOPE), fill), a,
                               np.full((T, LANES - MLA_NOPE - MLA_ROPE), fill)], axis=-1)

    cm, su, sd = half(MLA_ROPE // 4)
    return jnp.asarray(np.stack(ret + [pad(cm, 1.0), pad(su, 0.0), pad(sd, 0.0)]).astype(np.float32))


def _stacked_weights(mla_w_uq, mla_w_ukv, ffn_w_in, ffn_w_out):
    L = mla_w_uq.shape[0]
    uq = mla_w_uq.reshape(L, MLA_Q_RANK, MLA_HEADS, MLA_NOPE + MLA_ROPE)
    zq = jnp.zeros((L, MLA_Q_RANK, MLA_HEADS, LANES - MLA_NOPE - MLA_ROPE), F32)
    wuq = jnp.concatenate([uq, zq], axis=-1).reshape(L, MLA_Q_RANK, MQ_W).astype(BF16)

    ukv = mla_w_ukv.reshape(L, MLA_KV_RANK, MLA_HEADS, MLA_NOPE + MLA_V)
    zk = jnp.zeros((L, MLA_KV_RANK, MLA_HEADS, LANES - MLA_NOPE), F32)
    uk_pad = jnp.concatenate([ukv[..., :MLA_NOPE], zk], axis=-1).reshape(L, MLA_KV_RANK, MQ_W)
    uv = ukv[..., MLA_NOPE:].reshape(L, MLA_KV_RANK, MLA_HEADS * MLA_V)
    wukv = jnp.concatenate([uk_pad, uv], axis=2).astype(BF16)
    return wuq, wukv, ffn_w_in.astype(BF16), ffn_w_out.astype(BF16)


def _pack_vectors(*vectors):
    packed = jnp.concatenate(vectors, axis=-1)[:, None, :]
    blocks, off = [], 0
    for v in vectors:
        w = v.shape[-1]
        assert off % w == 0
        blocks.append((off // w, w))
        off += w
    return lambda layer: [_LayerLanes(packed, layer, blk, w) for blk, w in blocks]


def kernel(x, c, ctx, c_ctx, mod_w, mod_b, pre1_g, post1_g, pre2_g, post2_g, w_in, conv_w, conv_b,
           conv_ln_g, conv_ln_b, ret_log_decay, ret_gn_g, mla_q_norm_g, mla_w_uq, mla_kv_norm_g,
           mla_w_ukv, w_out, ffn_w_in, ffn_w_out):
    B, T, D = x.shape
    TC = ctx.shape[1]
    cv = jnp.concatenate([c, c_ctx[None], jnp.zeros((MOD_ROWS - B - 1, D), F32)], axis=0)
    mod_table = _modulation(cv, mod_w, mod_b).reshape(DEPTH * MOD_ROWS * 6, 1, D)

    tabs_lat = _rope_tables(T, True)
    tabs_ctx = _rope_tables(TC, False)
    zstate = jnp.zeros((B, RET_HEADS // 2, LANES, LANES), F32)

    stacks = (w_in, w_out) + _stacked_weights(mla_w_uq, mla_w_ukv, ffn_w_in, ffn_w_out)
    stacks += (jnp.pad(conv_w, ((0, 0), (0, 1), (0, 0))),)
    lg_lanes = jnp.repeat(ret_log_decay, RET_DK, axis=-1)
    vectors = _pack_vectors(pre1_g, post1_g, pre2_g, post2_g, mla_q_norm_g, ret_gn_g, lg_lanes[:, 0],
                            lg_lanes[:, 1], conv_b, conv_ln_g, conv_ln_b, mla_kv_norm_g)

    xc = ctx
    for l in range(DEPTH):
        last = l == DEPTH - 1
        w_ext, wo, wuq, wukv, w1, w2, cw = [_Layered(a, l) for a in stacks]
        pre1, post1, pre2, post2, qng, gng, lgf, lgb, cb, clg, clb, kvg = vectors(l)
        lat = _ModRows(mod_table, l * MOD_ROWS * 6, 6)
        cx = _ModRows(mod_table, (l * MOD_ROWS + B) * 6, 0)
        lgd = ret_log_decay[l]

        proj = lambda xx, md, tabs: _inproj(xx, md, pre1, w_ext, qng, wuq, kvg, wukv, tabs)
        aL, rqL, rkL, rvL, rgL, mqL, mkL, mveL, mvoL = proj(x, lat, tabs_lat)
        aC, rqC, rkC, rvC, rgC, mqC, mkC, mveC, mvoC = proj(xc, cx, tabs_ctx)

        yrC, sf, sb = _retention(lgd, rqC, rkC, rvC, rgC, gng, lgf, lgb, zstate, zstate)
        yrL, _, _ = _retention(lgd, rqL, rkL, rvL, rgL, gng, lgf, lgb, sf, sb)

        attL = _attention(mqL, [(mkC, mveC, mvoC), (mkL, mveL, mvoL)])
        mix = lambda a, yr, att, md, xx: _mixer(a, cw, cb, clg, clb, yr, att, wo, post1, md, xx)
        ffn = lambda xx, md: _ffn(xx, md, pre2, w1, w2, post2)
        x = mix(aL, yrL, attL, lat, x)
        if not last:
            attC = _attention(mqC, [(mkC, mveC, mvoC)])
            xc = mix(aC, yrC, attC, cx, xc)
        x = ffn(x, lat)
        if not last:
            xc = ffn(xc, cx)
    return x
```

```python
import functools

import numpy as np
import jax
import jax.numpy as jnp
from jax import lax
from jax.experimental import pallas as pl
from jax.experimental.pallas import tpu as pltpu

F32 = jnp.float32
BF16 = jnp.bfloat16

D_MODEL = 1024
DEPTH = 2
GRID_W = 64
EPS = 1e-6
LOG2E = 1.4426950408889634
ROPE_BASE = 10000.0
CONV_CH = 256
CONV_K = 31
RET_HEADS = 4
RET_DK = 64
RET_DV = 64
MLA_HEADS = 8
MLA_Q_RANK = 256
MLA_KV_RANK = 128
MLA_NOPE = 64
MLA_ROPE = 32
MLA_V = 64
D_FF = 2816

LANES = 128
SUBLANES = 8
HALO = 16
CONV_SPAN = 24
ROW_TILE = 512
FFN_TILE = 512
RET_CHUNK = 256
ATT_TQ = 512
ATT_TK = 512
ATT_AHEAD = 2
ATT_SLOTS = ATT_AHEAD + 1
ATT_ROWS = 32
MXU_DIM = 256
FF_CHUNKS = ((0, 6 * MXU_DIM), (6 * MXU_DIM, D_FF))
VMEM_LIMIT = 56 * 1024 * 1024

_OFF_A = 0
_OFF_Q = 512
_OFF_K = 768
_OFF_V = 1024
_OFF_G = 1280
_OFF_CQ = 1536
_OFF_CKV = 1792
_OFF_KRP = 1920
D_EXT = 2048
MOD_ROWS = 8
MQ_W = MLA_HEADS * LANES


def _sigmoid(x):
    return 1.0 / (1.0 + jnp.exp(-x))


def _silu(x):
    return x * _sigmoid(x)


def _params(*sem):
    return pltpu.CompilerParams(dimension_semantics=sem, vmem_limit_bytes=VMEM_LIMIT)


class _Layered:
    def __init__(self, arr, layer):
        self.arr, self.layer = arr, layer

    @property
    def spec(self):
        n, layer = self.arr.ndim - 1, self.layer
        return pl.BlockSpec((None,) + self.arr.shape[1:], lambda *_: (layer,) + (0,) * n,
                            pipeline_mode=pl.Buffered(1))


class _LayerLanes:
    def __init__(self, arr, layer, block, width):
        self.arr, self.layer, self.block, self.width = arr, layer, block, width

    @property
    def spec(self):
        layer, block = self.layer, self.block
        return pl.BlockSpec((None, 1, self.width), lambda *_: (layer, 0, block), pipeline_mode=pl.Buffered(1))


class _ModRows:
    def __init__(self, table, base, per_batch):
        self.table, self.base, self.per_batch = table, base, per_batch

    def spec(self, j):
        base, step = self.base + j, self.per_batch
        return pl.BlockSpec((1, 1, self.table.shape[-1]), lambda b, i: (base + step * b, 0, 0))


def _mod_kernel(cv_ref, w_ref, b_ref, o_ref):
    s = _silu(cv_ref[...])
    o_ref[0] = jnp.dot(s.astype(BF16), w_ref[0].astype(BF16), preferred_element_type=F32) + b_ref[0]


def _modulation(cv, mod_w, mod_b):
    L, D, N = mod_w.shape
    tn = 1536
    return pl.pallas_call(
        _mod_kernel,
        out_shape=jax.ShapeDtypeStruct((L, 8, N), F32),
        grid=(L, N // tn),
        in_specs=[pl.BlockSpec((8, D), lambda l, j: (0, 0)),
                  pl.BlockSpec((1, D, tn), lambda l, j: (l, 0, j)),
                  pl.BlockSpec((1, 1, tn), lambda l, j: (l, 0, j))],
        out_specs=pl.BlockSpec((1, 8, tn), lambda l, j: (l, 0, j)),
        compiler_params=_params("arbitrary", "arbitrary"),
        name="modulation",
    )(cv, mod_w, mod_b.reshape(L, 1, N))


def _rope(x, cos, sin_up, sin_dn, f):
    return x * cos + pltpu.roll(x, LANES - f, 1) * sin_up + pltpu.roll(x, f, 1) * sin_dn


def _inproj_kernel(x_ref, sh_ref, sc_ref, g_ref, w_ref, qng_ref, wuq_ref, kvg_ref, wukv_ref, tab_ref,
                   a_ref, rq_ref, rk_ref, rv_ref, rg_ref, mq_ref, mk_ref, mve_ref, mvo_ref):
    x = x_ref[0]
    h = x * lax.rsqrt(jnp.mean(x * x, axis=-1, keepdims=True) + EPS) * g_ref[...]
    hb = (h * (1.0 + sc_ref[0]) + sh_ref[0]).astype(BF16)

    def proj(lo, hi):
        return jnp.dot(hb, w_ref[:, lo:hi], preferred_element_type=F32)

    pm = proj(_OFF_CQ, D_EXT)
    pr = proj(_OFF_Q, _OFF_V)
    cq = pm[:, 0:MLA_Q_RANK]
    qn = cq * lax.rsqrt(jnp.mean(cq * cq, axis=-1, keepdims=True) + EPS) * qng_ref[...]
    qq = jnp.dot(qn.astype(BF16), wuq_ref[...], preferred_element_type=F32)
    ckv = pm[:, _OFF_CKV - _OFF_CQ:_OFF_KRP - _OFF_CQ]
    kvn = ckv * lax.rsqrt(jnp.mean(ckv * ckv, axis=-1, keepdims=True) + EPS) * kvg_ref[...]
    kv = jnp.dot(kvn.astype(BF16), wukv_ref[...], preferred_element_type=F32)
    pa = proj(_OFF_A, _OFF_Q)
    pv = proj(_OFF_V, _OFF_CQ)

    a_ref[0] = pa[:, 0:CONV_CH] * _sigmoid(pa[:, CONV_CH:2 * CONV_CH])

    ret_tabs = (tab_ref[0], tab_ref[1], tab_ref[2], RET_DK // 4)
    for i in range(2):
        lo, hi = i * LANES, (i + 1) * LANES
        rq_ref[0, :, lo:hi] = _rope(pr[:, lo:hi], *ret_tabs).astype(BF16)
        rk_ref[0, :, lo:hi] = _rope(pr[:, 256 + lo:256 + hi], *ret_tabs).astype(BF16)
    rv_ref[0] = pv[:, 0:256].astype(BF16)
    rg_ref[0] = pv[:, 256:512]

    mla_tabs = (tab_ref[3], tab_ref[4], tab_ref[5], MLA_ROPE // 4)
    scale = float((MLA_NOPE + MLA_ROPE) ** -0.5 * LOG2E)
    kr_slot = _rope(pm[:, _OFF_KRP - _OFF_CQ:D_EXT - _OFF_CQ], *mla_tabs)
    for hh in range(MLA_HEADS):
        lo, hi = hh * LANES, (hh + 1) * LANES
        mq_ref[0, :, lo:hi] = (_rope(qq[:, lo:hi], *mla_tabs) * scale).astype(BF16)
        mk_ref[0, :, lo:hi] = (kv[:, lo:hi] + kr_slot).astype(BF16)
    v = kv[:, MQ_W:]
    lane = lax.broadcasted_iota(jnp.int32, v.shape, 1)
    lane = lane % LANES
    even = lane < MLA_V
    mve_ref[0] = jnp.where(even, v, jnp.where(lane == MLA_V, 1.0, 0.0)).astype(BF16)
    mvo_ref[0] = jnp.where(even, jnp.where(lane == 0, 1.0, 0.0), v).astype(BF16)


def _inproj(x, mod, g, w_ext, qng, wuq, kvg, wukv, tabs):
    B, L, D = x.shape
    tm = min(ROW_TILE, L)
    row = lambda w: pl.BlockSpec((1, tm, w), lambda b, i: (b, i, 0))
    outs = [(CONV_CH, F32), (256, BF16), (256, BF16), (256, BF16), (256, F32),
            (MQ_W, BF16), (MQ_W, BF16), (512, BF16), (512, BF16)]
    layered = [g, w_ext, qng, wuq, kvg, wukv]
    return pl.pallas_call(
        _inproj_kernel,
        out_shape=[jax.ShapeDtypeStruct((B, L, w), dt) for w, dt in outs],
        grid=(B, L // tm),
        in_specs=[row(D), mod.spec(0), mod.spec(1)] + [a.spec for a in layered]
        + [pl.BlockSpec((tabs.shape[0], tm, LANES), lambda b, i: (0, i, 0))],
        out_specs=[row(w) for w, _ in outs],
        compiler_params=_params("arbitrary", "arbitrary"),
        name="inproj",
    )(x, mod.table, mod.table, *[a.arr for a in layered], tabs)


def _ret_kernel(lgd_ref, q_ref, k_ref, v_ref, g_ref, gng_ref, lgf_ref, lgb_ref, sf0_ref, sb0_ref,
                y_ref, sfo_ref, sbo_ref, w_scr, dec_scr, kvf_scr, kvb_scr, sfs_scr, sbs_scr, *, n_chunks):
    C = RET_CHUNK
    N = n_chunks
    P = RET_HEADS // 2
    U = 4 if N % 4 == 0 else 1

    @pl.when(pl.program_id(0) == 0)
    def _():
        t = lax.broadcasted_iota(jnp.int32, (C, C), 0)
        m = lax.broadcasted_iota(jnp.int32, (C, C), 1)
        d = (t - m).astype(F32)
        for hh in range(RET_HEADS):
            w_scr[hh] = jnp.exp(jnp.where(d >= 0.0, lgd_ref[0, hh] * d, lgd_ref[1, hh] * (-d)))
        j = lax.broadcasted_iota(jnp.int32, (C, LANES), 0).astype(F32)
        for pi in range(P):
            lgf = lgf_ref[:, pi * LANES:(pi + 1) * LANES]
            lgb = lgb_ref[:, pi * LANES:(pi + 1) * LANES]
            dec_scr[pi, 0] = jnp.exp(lgf * (C - 1.0 - j))
            dec_scr[pi, 1] = jnp.exp(lgb * j)
            dec_scr[pi, 2] = jnp.exp(lgf * (j + 1.0))
            dec_scr[pi, 3] = jnp.exp(lgb * (C - j))

    first = lax.broadcasted_iota(jnp.int32, (C, LANES), 1) < RET_DV
    r128 = lax.broadcasted_iota(jnp.int32, (LANES, LANES), 0)
    c128 = lax.broadcasted_iota(jnp.int32, (LANES, LANES), 1)
    blockdiag = (r128 < RET_DK) == (c128 < RET_DV)
    dn = (((1,), (1,)), ((), ()))

    def sums_one(i, pi):
        rows = pl.ds(pl.multiple_of(i * C, C), C)
        lo, hi = pi * LANES, (pi + 1) * LANES
        kf = k_ref[0, rows, lo:hi].astype(F32)
        vv = v_ref[0, rows, lo:hi]
        kzf = (kf * dec_scr[pi, 0]).T.astype(BF16)
        kzb = (kf * dec_scr[pi, 1]).T.astype(BF16)
        kvf_scr[pi, i] = jnp.where(blockdiag, jnp.dot(kzf, vv, preferred_element_type=F32), 0.0)
        kvb_scr[pi, i] = jnp.where(blockdiag, jnp.dot(kzb, vv, preferred_element_type=F32), 0.0)

    def out_one(i, pi):
        rows = pl.ds(pl.multiple_of(i * C, C), C)
        lo, hi = pi * LANES, (pi + 1) * LANES
        q = q_ref[0, rows, lo:hi]
        k = k_ref[0, rows, lo:hi]
        vv = v_ref[0, rows, lo:hi]
        zero = jnp.zeros_like(q)
        s0 = lax.dot_general(jnp.where(first, q, zero), k, dn, preferred_element_type=F32)
        s1 = lax.dot_general(jnp.where(first, zero, q), k, dn, preferred_element_type=F32)
        p0 = (s0 * w_scr[2 * pi]).astype(BF16)
        p1 = (s1 * w_scr[2 * pi + 1]).astype(BF16)
        o = jnp.dot(p0, jnp.where(first, vv, zero), preferred_element_type=F32)
        o = o + jnp.dot(p1, jnp.where(first, zero, vv), preferred_element_type=F32)
        o = o + jnp.dot(q, sfs_scr[pi, i].astype(BF16), preferred_element_type=F32) * dec_scr[pi, 2]
        o = o + jnp.dot(q, sbs_scr[pi, i].astype(BF16), preferred_element_type=F32) * dec_scr[pi, 3]
        inv = 1.0 / RET_DV
        tot = jnp.sum(o, axis=-1, keepdims=True)
        m0 = jnp.sum(jnp.where(first, o, 0.0), axis=-1, keepdims=True)
        mu = jnp.where(first, m0, tot - m0) * inv
        dlt = o - mu
        d2 = dlt * dlt
        tot2 = jnp.sum(d2, axis=-1, keepdims=True)
        v0 = jnp.sum(jnp.where(first, d2, 0.0), axis=-1, keepdims=True)
        var = jnp.where(first, v0, tot2 - v0) * inv
        y = dlt * lax.rsqrt(var + EPS) * gng_ref[:, lo:hi]
        y_ref[0, rows, lo:hi] = (y * _silu(g_ref[0, rows, lo:hi])).astype(BF16)

    def over_chunks(fn):
        def body(t, carry):
            for u in range(U):
                for pi in range(P):
                    fn(t * U + u, pi)
            return carry
        lax.fori_loop(0, N // U, body, 0)

    over_chunks(sums_one)

    gc_f = [jnp.exp(lgf_ref[:, pi * LANES:(pi + 1) * LANES] * float(C)) for pi in range(P)]
    gc_b = [jnp.exp(lgb_ref[:, pi * LANES:(pi + 1) * LANES] * float(C)) for pi in range(P)]

    def fwd_scan(i, ss):
        for pi in range(P):
            sfs_scr[pi, i] = ss[pi]
        return tuple(ss[pi] * gc_f[pi] + kvf_scr[pi, i] for pi in range(P))

    def bwd_scan(ii, ss):
        i = N - 1 - ii
        for pi in range(P):
            sbs_scr[pi, i] = ss[pi]
        return tuple(ss[pi] * gc_b[pi] + kvb_scr[pi, i] for pi in range(P))

    sf = lax.fori_loop(0, N, fwd_scan, tuple(sf0_ref[0, pi] for pi in range(P)))
    sb = lax.fori_loop(0, N, bwd_scan, tuple(sb0_ref[0, pi] for pi in range(P)))
    for pi in range(P):
        sfo_ref[0, pi] = sf[pi]
        sbo_ref[0, pi] = sb[pi]

    over_chunks(out_one)


def _retention(lgd, q, k, v, g, gng, lgf, lgb, sf0, sb0):
    B, L, W = q.shape
    C = RET_CHUNK
    N = L // C
    P = RET_HEADS // 2
    seq = lambda: pl.BlockSpec((1, L, W), lambda b: (b, 0, 0))
    st = lambda: pl.BlockSpec((1, P, LANES, LANES), lambda b: (b, 0, 0, 0))
    st_shape = jax.ShapeDtypeStruct((B, P, LANES, LANES), F32)
    return pl.pallas_call(
        functools.partial(_ret_kernel, n_chunks=N),
        out_shape=[jax.ShapeDtypeStruct((B, L, W), BF16), st_shape, st_shape],
        grid=(B,),
        in_specs=[pl.BlockSpec(memory_space=pltpu.SMEM), seq(), seq(), seq(), seq(),
                  gng.spec, lgf.spec, lgb.spec, st(), st()],
        out_specs=[seq(), st(), st()],
        scratch_shapes=[pltpu.VMEM((RET_HEADS, C, C), F32), pltpu.VMEM((P, 4, C, LANES), F32)]
        + [pltpu.VMEM((P, N, LANES, LANES), F32) for _ in range(4)],
        compiler_params=_params("arbitrary"),
        name="retention",
    )(lgd, q, k, v, g, gng.arr, lgf.arr, lgb.arr, sf0, sb0)


def _attn_kernel(*refs, lengths):
    q_ref = refs[0]
    src = refs[1:1 + 3 * len(lengths)]
    o_ref, s_scr, p_scr, m_scr, a_scr, acc_scr = refs[1 + 3 * len(lengths):]
    tq = q_ref.shape[1]
    dn = (((1,), (1,)), ((), ()))
    H = MLA_HEADS
    RB = ATT_ROWS

    m_scr[...] = jnp.full(m_scr.shape, -1e30, F32)
    acc_scr[...] = jnp.zeros(acc_scr.shape, F32)

    def step(k_ref, ve_ref, vo_ref, rows, tk):
        def scores(h):
            q = q_ref[0, :, h * LANES:(h + 1) * LANES]
            k = k_ref[0, rows, h * LANES:(h + 1) * LANES]
            s_scr[h % ATT_SLOTS, :, 0:tk] = lax.dot_general(q, k, dn, preferred_element_type=F32)

        for h in range(ATT_AHEAD):
            scores(h)
        for h in range(H):
            if h + ATT_AHEAD < H:
                scores(h + ATT_AHEAD)
            for r0 in range(0, tq, RB):
                s = s_scr[h % ATT_SLOTS, r0:r0 + RB, 0:tk]
                m_old = m_scr[h, r0:r0 + RB, :]
                n = jnp.maximum(m_old, jnp.max(s, axis=-1, keepdims=True))
                p_scr[h % 2, r0:r0 + RB, 0:tk] = jnp.exp2((s - jnp.tile(n, (1, tk // LANES))).astype(BF16))
                a_scr[h % 2, r0:r0 + RB, :] = jnp.exp2(m_old - n)
                m_scr[h, r0:r0 + RB, :] = n
            v_ref = ve_ref if h % 2 == 0 else vo_ref
            v = v_ref[0, rows, (h // 2) * LANES:(h // 2 + 1) * LANES]
            acc_scr[h] = acc_scr[h] * a_scr[h % 2] + jnp.dot(p_scr[h % 2, :, 0:tk], v, preferred_element_type=F32)

    for si, length in enumerate(lengths):
        k_ref, ve_ref, vo_ref = src[3 * si:3 * si + 3]
        tk = min(ATT_TK, length)
        n = length // tk
        if n == 1:
            step(k_ref, ve_ref, vo_ref, pl.ds(0, tk), tk)
        else:
            @pl.loop(0, n)
            def _(c):
                step(k_ref, ve_ref, vo_ref, pl.ds(pl.multiple_of(c * tk, tk), tk), tk)

    first = lax.broadcasted_iota(jnp.int32, (tq, LANES), 1) < MLA_V
    for pj in range(H // 2):
        a0, a1 = acc_scr[2 * pj], acc_scr[2 * pj + 1]
        out = jnp.where(first, a0 / a0[:, MLA_V:MLA_V + 1], a1 / a1[:, 0:1])
        o_ref[0, :, pj * LANES:(pj + 1) * LANES] = out.astype(BF16)


def _attention(q, sources):
    B, L, _ = q.shape
    tq = min(ATT_TQ, L)
    lengths =tuple(s[0].shape[1] for s in sources)
    in_specs = [pl.BlockSpec((1, tq, MQ_W), lambda b, i: (b, i, 0))]
    args = [q]
    for k, ve, vo in sources:
        S = k.shape[1]
        in_specs += [pl.BlockSpec((1, S, MQ_W), lambda b, i: (b, 0, 0)),
                     pl.BlockSpec((1, S, 512), lambda b, i: (b, 0, 0)),
                     pl.BlockSpec((1, S, 512), lambda b, i: (b, 0, 0))]
        args += [k, ve, vo]
    return pl.pallas_call(
        functools.partial(_attn_kernel, lengths=lengths),
        out_shape=jax.ShapeDtypeStruct((B, L, MLA_HEADS * MLA_V), BF16),
        grid=(B, L // tq),
        in_specs=in_specs,
        out_specs=pl.BlockSpec((1, tq, MLA_HEADS * MLA_V), lambda b, i: (b, i, 0)),
        scratch_shapes=[pltpu.VMEM((ATT_SLOTS, tq, ATT_TK), F32),
                        pltpu.VMEM((2, tq, ATT_TK), BF16),
                        pltpu.VMEM((MLA_HEADS, tq, LANES), F32),
                        pltpu.VMEM((2, tq, LANES), F32),
                        pltpu.VMEM((MLA_HEADS, tq, LANES), F32)],
        compiler_params=_params("arbitrary", "arbitrary"),
        name="attention",
    )(*args)


def _mixer_kernel(a_ref, ap_ref, an_ref, cw_ref, cb_ref, lg_ref, lb_ref, yr_ref, at_ref, wo_ref,
                  pg_ref, gate_ref, x_ref, o_ref, win_scr, shf_scr, yc_scr):
    tm = a_ref.shape[1]
    i = pl.program_id(1)
    n = pl.num_programs(1)
    win_scr[0:HALO] = jnp.where(i > 0, ap_ref[0], 0.0)
    win_scr[HALO:HALO + tm] = a_ref[0]
    win_scr[HALO + tm:HALO + tm + HALO] = jnp.where(i < n - 1, an_ref[0], 0.0)
    span = tm + CONV_SPAN
    for ph in range(1, SUBLANES):
        shf_scr[ph - 1] = win_scr[ph:ph + span, :]
    sub = 64
    off = HALO - CONV_K // 2
    for r0 in range(0, tm, sub):
        acc = jnp.zeros((sub, CONV_CH), F32) + cb_ref[...]
        for t in range(CONV_K):
            ph = (off + t) % SUBLANES
            base = r0 + off + t - ph
            rows = win_scr[base:base + sub, :] if ph == 0 else shf_scr[ph - 1, base:base + sub, :]
            acc = acc + cw_ref[t:t + 1, :] * rows
        mu = jnp.mean(acc, axis=-1, keepdims=True)
        d = acc - mu
        var = jnp.mean(d * d, axis=-1, keepdims=True)
        yc = d * lax.rsqrt(var + EPS) * lg_ref[...] + lb_ref[...]
        yc_scr[r0:r0 + sub, :] = _silu(yc).astype(BF16)
    y = jnp.dot(yc_scr[...], wo_ref[0:256, :], preferred_element_type=F32)
    y = y + jnp.dot(yr_ref[0], wo_ref[256:512, :], preferred_element_type=F32)
    y = y + jnp.dot(at_ref[0], wo_ref[512:1024, :], preferred_element_type=F32)
    r = y * lax.rsqrt(jnp.mean(y * y, axis=-1, keepdims=True) + EPS) * pg_ref[...]
    o_ref[0] = x_ref[0] + gate_ref[0] * r


def _mixer(a, cw, cb, lg, lb, yr, att, wo, pg, mod, x):
    B, L, D = x.shape
    tm = min(ROW_TILE, L)
    hb = tm // HALO
    nh = L // HALO
    row = lambda w: pl.BlockSpec((1, tm, w), lambda b, i: (b, i, 0))
    return pl.pallas_call(
        _mixer_kernel,
        out_shape=jax.ShapeDtypeStruct((B, L, D), F32),
        grid=(B, L // tm),
        in_specs=[row(CONV_CH),
                  pl.BlockSpec((1, HALO, CONV_CH), lambda b, i: (b, jnp.maximum(i * hb - 1, 0), 0)),
                  pl.BlockSpec((1, HALO, CONV_CH), lambda b, i: (b, jnp.minimum((i + 1) * hb, nh - 1), 0)),
                  cw.spec, cb.spec, lg.spec, lb.spec, row(256), row(512), wo.spec, pg.spec, mod.spec(2), row(D)],
        out_specs=row(D),
        scratch_shapes=[pltpu.VMEM((tm + 2 * HALO, CONV_CH), F32),
                        pltpu.VMEM((SUBLANES - 1, tm + CONV_SPAN, CONV_CH), F32),
                        pltpu.VMEM((tm, CONV_CH), BF16)],
        compiler_params=_params("arbitrary", "arbitrary"),
        name="mixer_out",
    )(a, a, a, cw.arr, cb.arr, lg.arr, lb.arr, yr, att, wo.arr, pg.arr, mod.table, x)


def _ffn_kernel(x_ref, sh_ref, sc_ref, g_ref, w1_ref, w2_ref, pg_ref, gate_ref, o_ref):
    x = x_ref[0]
    h = x * lax.rsqrt(jnp.mean(x * x, axis=-1, keepdims=True) + EPS) * g_ref[...]
    h = (h * (1.0 + sc_ref[0]) + sh_ref[0]).astype(BF16)
    y = jnp.zeros(x.shape, F32)
    for lo, hi in FF_CHUNKS:
        u = jnp.dot(h, w1_ref[:, lo:hi], preferred_element_type=F32)
        gt = jnp.dot(h, w1_ref[:, D_FF + lo:D_FF + hi], preferred_element_type=F32)
        y = y + jnp.dot((_silu(gt) * u).astype(BF16), w2_ref[lo:hi, :], preferred_element_type=F32)
    r = y * lax.rsqrt(jnp.mean(y * y, axis=-1, keepdims=True) + EPS) * pg_ref[...]
    o_ref[0] = x + gate_ref[0] * r


def _ffn(x, mod, g, w1, w2, pg):
    B, L, D = x.shape
    tm = min(FFN_TILE, L)
    row = pl.BlockSpec((1, tm, D), lambda b, i: (b, i, 0))
    return pl.pallas_call(
        _ffn_kernel,
        out_shape=jax.ShapeDtypeStruct((B, L, D), F32),
        grid=(B, L // tm),
        in_specs=[row, mod.spec(3), mod.spec(4), g.spec, w1.spec, w2.spec, pg.spec, mod.spec(5)],
        out_specs=row,
        compiler_params=_params("arbitrary", "arbitrary"),
        name="ffn",
    )(x, mod.table, mod.table, g.arr, w1.arr, w2.arr, pg.arr, mod.table)


def _rope_tables(T, rotary):
    t = np.arange(T)
    rowp = (t // GRID_W).astype(np.float64)[:, None]
    colp = (t % GRID_W).astype(np.float64)[:, None]

    def half(f):
        inv = ROPE_BASE ** (-np.arange(f, dtype=np.float64) / f)
        ar, ac = rowp * inv[None, :], colp * inv[None, :]
        cos = np.concatenate([np.cos(ar)] * 2 + [np.cos(ac)] * 2, axis=-1)
        sin_r, sin_c, z = np.sin(ar), np.sin(ac), np.zeros_like(ar)
        sin_up = np.concatenate([-sin_r, z, -sin_c, z], axis=-1)
        sin_dn = np.concatenate([z, sin_r, z, sin_c], axis=-1)
        if not rotary:
            cos, sin_up, sin_dn = np.ones_like(cos), np.zeros_like(cos), np.zeros_like(cos)
        return cos, sin_up, sin_dn

    ret = [np.tile(a, (1, 2)) for a in half(RET_DK // 4)]

    def pad(a, fill):
        return np.concatenate([np.full((T, MLA_NOPE), fill), a,
                               np.full((T, LANES - MLA_NOPE - MLA_ROPE), fill)], axis=-1)

    cm, su, sd = half(MLA_ROPE // 4)
    return jnp.asarray(np.stack(ret + [pad(cm, 1.0), pad(su, 0.0), pad(sd, 0.0)]).astype(np.float32))


def _stacked_weights(w_in, mla_w_uq, mla_w_ukv, w_out, ffn_w_in, ffn_w_out):
    L, D, _ = w_in.shape
    offs = [0]
    for s in (2 * CONV_CH, 256, 256, 256, 256, MLA_Q_RANK, MLA_KV_RANK, MLA_ROPE):
        offs.append(offs[-1] + s)
    wa, wq, wk, wv, wg, wcq, wckv, wkr = [w_in[:, :, offs[i]:offs[i + 1]] for i in range(8)]
    zl = jnp.zeros((L, D, MLA_NOPE), F32)
    zr = jnp.zeros((L, D, LANES - MLA_NOPE - MLA_ROPE), F32)
    w_ext = jnp.concatenate([wa, wq, wk * (RET_DK ** -0.5), wv, wg, wcq, wckv, zl, wkr, zr], axis=2).astype(BF16)

    uq = mla_w_uq.reshape(L, MLA_Q_RANK, MLA_HEADS, MLA_NOPE + MLA_ROPE)
    zq = jnp.zeros((L, MLA_Q_RANK, MLA_HEADS, LANES - MLA_NOPE - MLA_ROPE), F32)
    wuq = jnp.concatenate([uq, zq], axis=-1).reshape(L, MLA_Q_RANK, MQ_W).astype(BF16)

    ukv = mla_w_ukv.reshape(L, MLA_KV_RANK, MLA_HEADS, MLA_NOPE + MLA_V)
    zk = jnp.zeros((L, MLA_KV_RANK, MLA_HEADS, LANES - MLA_NOPE), F32)
    uk_pad = jnp.concatenate([ukv[..., :MLA_NOPE], zk], axis=-1).reshape(L, MLA_KV_RANK, MQ_W)
    uv = ukv[..., MLA_NOPE:].reshape(L, MLA_KV_RANK, MLA_HEADS * MLA_V)
    wukv = jnp.concatenate([uk_pad, uv], axis=2).astype(BF16)
    return w_ext, wuq, wukv, w_out.astype(BF16), ffn_w_in.astype(BF16), ffn_w_out.astype(BF16)


def _pack_vectors(*vectors):
    packed = jnp.concatenate(vectors, axis=-1)[:, None, :]
    blocks, off = [], 0
    for v in vectors:
        w = v.shape[-1]
        assert off % w == 0
        blocks.append((off // w, w))
        off += w
    return lambda layer: [_LayerLanes(packed, layer, blk, w) for blk, w in blocks]


def kernel(x, c, ctx, c_ctx, mod_w, mod_b, pre1_g, post1_g, pre2_g, post2_g, w_in, conv_w, conv_b,
           conv_ln_g, conv_ln_b, ret_log_decay, ret_gn_g, mla_q_norm_g, mla_w_uq, mla_kv_norm_g,
           mla_w_ukv, w_out, ffn_w_in, ffn_w_out):
    B, T, D = x.shape
    TC = ctx.shape[1]
    cv = jnp.concatenate([c, c_ctx[None], jnp.zeros((MOD_ROWS - B - 1, D), F32)], axis=0)
    mod_table = _modulation(cv, mod_w, mod_b).reshape(DEPTH * MOD_ROWS * 6, 1, D)

    tabs_lat = _rope_tables(T, True)
    tabs_ctx = _rope_tables(TC, False)
    zstate = jnp.zeros((B, RET_HEADS // 2, LANES, LANES), F32)

    stacks = _stacked_weights(w_in, mla_w_uq, mla_w_ukv, w_out, ffn_w_in, ffn_w_out)
    stacks += (jnp.pad(conv_w, ((0, 0), (0, 1), (0, 0))),)
    lg_lanes = jnp.repeat(ret_log_decay, RET_DK, axis=-1)
    vectors = _pack_vectors(pre1_g, post1_g, pre2_g, post2_g, mla_q_norm_g, ret_gn_g, lg_lanes[:, 0],
                            lg_lanes[:, 1], conv_b, conv_ln_g, conv_ln_b, mla_kv_norm_g)

    xc = ctx
    for l in range(DEPTH):
        last = l == DEPTH - 1
        w_ext, wuq, wukv, wo, w1, w2, cw = [_Layered(a, l) for a in stacks]
        pre1, post1, pre2, post2, qng, gng, lgf, lgb, cb, clg, clb, kvg = vectors(l)
        lat = _ModRows(mod_table, l * MOD_ROWS * 6, 6)
        cx = _ModRows(mod_table, (l * MOD_ROWS + B) * 6, 0)
        lgd = ret_log_decay[l]

        proj = lambda xx, md, tabs: _inproj(xx, md, pre1, w_ext, qng, wuq, kvg, wukv, tabs)
        aL, rqL, rkL, rvL, rgL, mqL, mkL, mveL, mvoL = proj(x, lat, tabs_lat)
        aC, rqC, rkC, rvC, rgC, mqC, mkC, mveC, mvoC = proj(xc, cx, tabs_ctx)

        yrC, sf, sb = _retention(lgd, rqC, rkC, rvC, rgC, gng, lgf, lgb, zstate, zstate)
        yrL, _, _ = _retention(lgd, rqL, rkL, rvL, rgL, gng, lgf, lgb, sf, sb)

        attL = _attention(mqL, [(mkC, mveC, mvoC), (mkL, mveL, mvoL)])
        mix = lambda a, yr, att, md, xx: _mixer(a, cw, cb, clg, clb, yr, att, wo, post1, md, xx)
        ffn = lambda xx, md: _ffn(xx, md, pre2, w1, w2, post2)
        x = mix(aL, yrL, attL, lat, x)
        if not last:
            attC = _attention(mqC, [(mkC, mveC, mvoC)])
            xc = mix(aC, yrC, attC, cx, xc)
        x = ffn(x, lat)
        if not last:
            xc = ffn(xc, cx)
    return x
```

```python
import functools

import numpy as np
import jax
import jax.numpy as jnp
from jax import lax
from jax.experimental import pallas as pl
from jax.experimental.pallas import tpu as pltpu

F32 = jnp.float32
BF16 = jnp.bfloat16

D_MODEL = 1024
DEPTH = 2
GRID_W = 64
EPS = 1e-6
LOG2E = 1.4426950408889634
ROPE_BASE = 10000.0
CONV_CH = 256
CONV_K = 31
RET_HEADS = 4
RET_DK = 64
RET_DV = 64
MLA_HEADS = 8
MLA_Q_RANK = 256
MLA_KV_RANK = 128
MLA_NOPE = 64
MLA_ROPE = 32
MLA_V = 64
D_FF = 2816

LANES = 128
SUBLANES = 8
HALO = 16
CONV_SPAN = 24
ROW_TILE = 512
FFN_TILE = 512
RET_CHUNK = 256
ATT_TQ = 512
ATT_TK = 512
ATT_AHEAD = 2
ATT_SLOTS = ATT_AHEAD + 1
ATT_ROWS = 32
MXU_DIM = 256
FF_CHUNKS = ((0, 6 * MXU_DIM), (6 * MXU_DIM, D_FF))
VMEM_LIMIT = 56 * 1024 * 1024

_OFF_A = 0
_OFF_Q = 512
_OFF_K = 768
_OFF_V = 1024
_OFF_G = 1280
_OFF_CQ = 1536
_OFF_CKV = 1792
_OFF_KRP = 1920
D_EXT = 2048
MOD_ROWS = 8
MQ_W = MLA_HEADS * LANES


def _sigmoid(x):
    return 1.0 / (1.0 + jnp.exp(-x))


def _silu(x):
    return x * _sigmoid(x)


def _params(*sem):
    return pltpu.CompilerParams(dimension_semantics=sem, vmem_limit_bytes=VMEM_LIMIT)


class _Layered:
    def __init__(self, arr, layer):
        self.arr, self.layer = arr, layer

    @property
    def spec(self):
        n, layer = self.arr.ndim - 1, self.layer
        return pl.BlockSpec((None,) + self.arr.shape[1:], lambda *_: (layer,) + (0,) * n,
                            pipeline_mode=pl.Buffered(1))


class _LayerLanes:
    def __init__(self, arr, layer, block, width):
        self.arr, self.layer, self.block, self.width = arr, layer, block, width

    @property
    def spec(self):
        layer, block = self.layer, self.block
        return pl.BlockSpec((None, 1, self.width), lambda *_: (layer, 0, block), pipeline_mode=pl.Buffered(1))


class _ModRows:
    def __init__(self, table, base, per_batch):
        self.table, self.base, self.per_batch = table, base, per_batch

    def spec(self, j):
        base, step = self.base + j, self.per_batch
        return pl.BlockSpec((1, 1, self.table.shape[-1]), lambda b, i: (base + step * b, 0, 0))


def _mod_kernel(cv_ref, w_ref, b_ref, o_ref):
    s = _silu(cv_ref[...])
    o_ref[0] = jnp.dot(s.astype(BF16), w_ref[0].astype(BF16), preferred_element_type=F32) + b_ref[0]


def _modulation(cv, mod_w, mod_b):
    L, D, N = mod_w.shape
    tn = 1536
    return pl.pallas_call(
        _mod_kernel,
        out_shape=jax.ShapeDtypeStruct((L, 8, N), F32),
        grid=(L, N // tn),
        in_specs=[pl.BlockSpec((8, D), lambda l, j: (0, 0)),
                  pl.BlockSpec((1, D, tn), lambda l, j: (l, 0, j)),
                  pl.BlockSpec((1, 1, tn), lambda l, j: (l, 0, j))],
        out_specs=pl.BlockSpec((1, 8, tn), lambda l, j: (l, 0, j)),
        compiler_params=_params("arbitrary", "arbitrary"),
        name="modulation",
    )(cv, mod_w, mod_b.reshape(L, 1, N))


def _rope(x, cos, sin_up, sin_dn, f):
    return x * cos + pltpu.roll(x, LANES - f, 1) * sin_up + pltpu.roll(x, f, 1) * sin_dn


def _inproj_kernel(x_ref, sh_ref, sc_ref, g_ref, w_ref, qng_ref, wuq_ref, kvg_ref, wukv_ref, tab_ref,
                   a_ref, rq_ref, rk_ref, rv_ref, rg_ref, mq_ref, mk_ref, mve_ref, mvo_ref):
    x = x_ref[0]
    h = x * lax.rsqrt(jnp.mean(x * x, axis=-1, keepdims=True) + EPS) * g_ref[...]
    hb = (h * (1.0 + sc_ref[0]) + sh_ref[0]).astype(BF16)

    def proj(lo, hi):
        return jnp.dot(hb, w_ref[:, lo:hi], preferred_element_type=F32)

    pm = proj(_OFF_CQ, D_EXT)
    pr = proj(_OFF_Q, _OFF_V)
    cq = pm[:, 0:MLA_Q_RANK]
    qn = cq * lax.rsqrt(jnp.mean(cq * cq, axis=-1, keepdims=True) + EPS) * qng_ref[...]
    qq = jnp.dot(qn.astype(BF16), wuq_ref[...], preferred_element_type=F32)
    ckv = pm[:, _OFF_CKV - _OFF_CQ:_OFF_KRP - _OFF_CQ]
    kvn = ckv * lax.rsqrt(jnp.mean(ckv * ckv, axis=-1, keepdims=True) + EPS) * kvg_ref[...]
    kv = jnp.dot(kvn.astype(BF16), wukv_ref[...], preferred_element_type=F32)
    pa = proj(_OFF_A, _OFF_Q)
    pv = proj(_OFF_V, _OFF_CQ)

    a_ref[0] = pa[:, 0:CONV_CH] * _sigmoid(pa[:, CONV_CH:2 * CONV_CH])

    ret_tabs = (tab_ref[0], tab_ref[1], tab_ref[2], RET_DK // 4)
    for i in range(2):
        lo, hi = i * LANES, (i + 1) * LANES
        rq_ref[0, :, lo:hi] = _rope(pr[:, lo:hi], *ret_tabs).astype(BF16)
        rk_ref[0, :, lo:hi] = _rope(pr[:, 256 + lo:256 + hi], *ret_tabs).astype(BF16)
    rv_ref[0] = pv[:, 0:256].astype(BF16)
    rg_ref[0] = pv[:, 256:512]

    mla_tabs = (tab_ref[3], tab_ref[4], tab_ref[5], MLA_ROPE // 4)
    scale = float((MLA_NOPE + MLA_ROPE) ** -0.5 * LOG2E)
    kr_slot = _rope(pm[:, _OFF_KRP - _OFF_CQ:D_EXT - _OFF_CQ], *mla_tabs)
    for hh in range(MLA_HEADS):
        lo, hi = hh * LANES, (hh + 1) * LANES
        mq_ref[0, :, lo:hi] = (_rope(qq[:, lo:hi], *mla_tabs) * scale).astype(BF16)
        mk_ref[0, :, lo:hi] = (kv[:, lo:hi] + kr_slot).astype(BF16)
    v = kv[:, MQ_W:]
    lane = lax.broadcasted_iota(jnp.int32, v.shape, 1)
    lane = lane % LANES
    even = lane < MLA_V
    mve_ref[0] = jnp.where(even, v, jnp.where(lane == MLA_V, 1.0, 0.0)).astype(BF16)
    mvo_ref[0] = jnp.where(even, jnp.where(lane == 0, 1.0, 0.0), v).astype(BF16)


def _inproj(x, mod, g, w_ext, qng, wuq, kvg, wukv, tabs):
    B, L, D = x.shape
    tm = min(ROW_TILE, L)
    row = lambda w: pl.BlockSpec((1, tm, w), lambda b, i: (b, i, 0))
    outs = [(CONV_CH, F32), (256, BF16), (256, BF16), (256, BF16), (256, F32),
            (MQ_W, BF16), (MQ_W, BF16), (512, BF16), (512, BF16)]
    layered = [g, w_ext, qng, wuq, kvg, wukv]
    return pl.pallas_call(
        _inproj_kernel,
        out_shape=[jax.ShapeDtypeStruct((B, L, w), dt) for w, dt in outs],
        grid=(B, L // tm),
        in_specs=[row(D), mod.spec(0), mod.spec(1)] + [a.spec for a in layered]
        + [pl.BlockSpec((tabs.shape[0], tm, LANES), lambda b, i: (0, i, 0))],
        out_specs=[row(w) for w, _ in outs],
        compiler_params=_params("arbitrary", "arbitrary"),
        name="inproj",
    )(x, mod.table, mod.table, *[a.arr for a in layered], tabs)


def _ret_kernel(lgd_ref, q_ref, k_ref, v_ref, g_ref, gng_ref, lgf_ref, lgb_ref, sf0_ref, sb0_ref,
                y_ref, sfo_ref, sbo_ref, w_scr, dec_scr, kvf_scr, kvb_scr, sfs_scr, sbs_scr, *, n_chunks):
    C = RET_CHUNK
    N = n_chunks
    P = RET_HEADS // 2
    U = 4 if N % 4 == 0 else 1

    @pl.when(pl.program_id(0) == 0)
    def _():
        t = lax.broadcasted_iota(jnp.int32, (C, C), 0)
        m = lax.broadcasted_iota(jnp.int32, (C, C), 1)
        d = (t - m).astype(F32)
        for hh in range(RET_HEADS):
            w_scr[hh] = jnp.exp(jnp.where(d >= 0.0, lgd_ref[0, hh] * d, lgd_ref[1, hh] * (-d)))
        j = lax.broadcasted_iota(jnp.int32, (C, LANES), 0).astype(F32)
        for pi in range(P):
            lgf = lgf_ref[:, pi * LANES:(pi + 1) * LANES]
            lgb = lgb_ref[:, pi * LANES:(pi + 1) * LANES]
            dec_scr[pi, 0] = jnp.exp(lgf * (C - 1.0 - j))
            dec_scr[pi, 1] = jnp.exp(lgb * j)
            dec_scr[pi, 2] = jnp.exp(lgf * (j + 1.0))
            dec_scr[pi, 3] = jnp.exp(lgb * (C - j))

    first = lax.broadcasted_iota(jnp.int32, (C, LANES), 1) < RET_DV
    r128 = lax.broadcasted_iota(jnp.int32, (LANES, LANES), 0)
    c128 = lax.broadcasted_iota(jnp.int32, (LANES, LANES), 1)
    blockdiag = (r128 < RET_DK) == (c128 < RET_DV)
    dn = (((1,), (1,)), ((), ()))

    def sums_one(i, pi):
        rows = pl.ds(pl.multiple_of(i * C, C), C)
        lo, hi = pi * LANES, (pi + 1) * LANES
        kf = k_ref[0, rows, lo:hi].astype(F32)
        vv = v_ref[0, rows, lo:hi]
        kzf = (kf * dec_scr[pi, 0]).T.astype(BF16)
        kzb = (kf * dec_scr[pi, 1]).T.astype(BF16)
        kvf_scr[pi, i] = jnp.where(blockdiag, jnp.dot(kzf, vv, preferred_element_type=F32), 0.0)
        kvb_scr[pi, i] = jnp.where(blockdiag, jnp.dot(kzb, vv, preferred_element_type=F32), 0.0)

    def out_one(i, pi):
        rows = pl.ds(pl.multiple_of(i * C, C), C)
        lo, hi = pi * LANES, (pi + 1) * LANES
        q = q_ref[0, rows, lo:hi]
        k = k_ref[0, rows, lo:hi]
        vv = v_ref[0, rows, lo:hi]
        zero = jnp.zeros_like(q)
        s0 = lax.dot_general(jnp.where(first, q, zero), k, dn, preferred_element_type=F32)
        s1 = lax.dot_general(jnp.where(first, zero, q), k, dn, preferred_element_type=F32)
        p0 = (s0 * w_scr[2 * pi]).astype(BF16)
        p1 = (s1 * w_scr[2 * pi + 1]).astype(BF16)
        o = jnp.dot(p0, jnp.where(first, vv, zero), preferred_element_type=F32)
        o = o + jnp.dot(p1, jnp.where(first, zero, vv), preferred_element_type=F32)
        o = o + jnp.dot(q, sfs_scr[pi, i].astype(BF16), preferred_element_type=F32) * dec_scr[pi, 2]
        o = o + jnp.dot(q, sbs_scr[pi, i].astype(BF16), preferred_element_type=F32) * dec_scr[pi, 3]
        inv = 1.0 / RET_DV
        tot = jnp.sum(o, axis=-1, keepdims=True)
        m0 = jnp.sum(jnp.where(first, o, 0.0), axis=-1, keepdims=True)
        mu = jnp.where(first, m0, tot - m0) * inv
        dlt = o - mu
        d2 = dlt * dlt
        tot2 = jnp.sum(d2, axis=-1, keepdims=True)
        v0 = jnp.sum(jnp.where(first, d2, 0.0), axis=-1, keepdims=True)
        var = jnp.where(first, v0, tot2 - v0) * inv
        y = dlt * lax.rsqrt(var + EPS) * gng_ref[:, lo:hi]
        y_ref[0, rows, lo:hi] = (y * _silu(g_ref[0, rows, lo:hi])).astype(BF16)

    def over_chunks(fn):
        def body(t, carry):
            for u in range(U):
                for pi in range(P):
                    fn(t * U + u, pi)
            return carry
        lax.fori_loop(0, N // U, body, 0)

    over_chunks(sums_one)

    gc_f = [jnp.exp(lgf_ref[:, pi * LANES:(pi + 1) * LANES] * float(C)) for pi in range(P)]
    gc_b = [jnp.exp(lgb_ref[:, pi * LANES:(pi + 1) * LANES] * float(C)) for pi in range(P)]

    def fwd_scan(i, ss):
        for pi in range(P):
            sfs_scr[pi, i] = ss[pi]
        return tuple(ss[pi] * gc_f[pi] + kvf_scr[pi, i] for pi in range(P))

    def bwd_scan(ii, ss):
        i = N - 1 - ii
        for pi in range(P):
            sbs_scr[pi, i] = ss[pi]
        return tuple(ss[pi] * gc_b[pi] + kvb_scr[pi, i] for pi in range(P))

    sf = lax.fori_loop(0, N, fwd_scan, tuple(sf0_ref[0, pi] for pi in range(P)))
    sb = lax.fori_loop(0, N, bwd_scan, tuple(sb0_ref[0, pi] for pi in range(P)))
    for pi in range(P):
        sfo_ref[0, pi] = sf[pi]
        sbo_ref[0, pi] = sb[pi]

    over_chunks(out_one)


def _retention(lgd, q, k, v, g, gng, lgf, lgb, sf0, sb0):
    B, L, W = q.shape
    C = RET_CHUNK
    N = L // C
    P = RET_HEADS // 2
    seq = lambda: pl.BlockSpec((1, L, W), lambda b: (b, 0, 0))
    st = lambda: pl.BlockSpec((1, P, LANES, LANES), lambda b: (b, 0, 0, 0))
    st_shape = jax.ShapeDtypeStruct((B, P, LANES, LANES), F32)
    return pl.pallas_call(
        functools.partial(_ret_kernel, n_chunks=N),
        out_shape=[jax.ShapeDtypeStruct((B, L, W), BF16), st_shape, st_shape],
        grid=(B,),
        in_specs=[pl.BlockSpec(memory_space=pltpu.SMEM), seq(), seq(), seq(), seq(),
                  gng.spec, lgf.spec, lgb.spec, st(), st()],
        out_specs=[seq(), st(), st()],
        scratch_shapes=[pltpu.VMEM((RET_HEADS, C, C), F32), pltpu.VMEM((P, 4, C, LANES), F32)]
        + [pltpu.VMEM((P, N, LANES, LANES), F32) for _ in range(4)],
        compiler_params=_params("arbitrary"),
        name="retention",
    )(lgd, q, k, v, g, gng.arr, lgf.arr, lgb.arr, sf0, sb0)


def _attn_kernel(*refs, lengths):
    q_ref = refs[0]
    src = refs[1:1 + 3 * len(lengths)]
    o_ref, s_scr, p_scr, m_scr, a_scr, acc_scr = refs[1 + 3 * len(lengths):]
    tq = q_ref.shape[1]
    dn = (((1,), (1,)), ((), ()))
    H = MLA_HEADS
    RB = ATT_ROWS

    m_scr[...] = jnp.full(m_scr.shape, -1e30, F32)
    acc_scr[...] = jnp.zeros(acc_scr.shape, F32)

    def step(k_ref, ve_ref, vo_ref, rows, tk):
        def scores(h):
            q = q_ref[0, :, h * LANES:(h + 1) * LANES]
            k = k_ref[0, rows, h * LANES:(h + 1) * LANES]
            s_scr[h % ATT_SLOTS, :, 0:tk] = lax.dot_general(q, k, dn, preferred_element_type=F32)

        for h in range(ATT_AHEAD):
            scores(h)
        for h in range(H):
            if h + ATT_AHEAD < H:
                scores(h + ATT_AHEAD)
            for r0 in range(0, tq, RB):
                s = s_scr[h % ATT_SLOTS, r0:r0 + RB, 0:tk]
                m_old = m_scr[h, r0:r0 + RB, :]
                n = jnp.maximum(m_old, jnp.max(s, axis=-1, keepdims=True))
                p_scr[h % 2, r0:r0 + RB, 0:tk] = jnp.exp2(s - jnp.tile(n, (1, tk // LANES))).astype(BF16)
                a_scr[h % 2, r0:r0 + RB, :] = jnp.exp2(m_old - n)
                m_scr[h, r0:r0 + RB, :] = n
            v_ref = ve_ref if h % 2 == 0 else vo_ref
            v = v_ref[0, rows, (h // 2) * LANES:(h // 2 + 1) * LANES]
            acc_scr[h] = acc_scr[h] * a_scr[h % 2] + jnp.dot(p_scr[h % 2, :, 0:tk], v, preferred_element_type=F32)

    for si, length in enumerate(lengths):
        k_ref, ve_ref, vo_ref = src[3 * si:3 * si + 3]
        tk = min(ATT_TK, length)
        n = length // tk
        if n == 1:
            step(k_ref, ve_ref, vo_ref, pl.ds(0, tk), tk)
        else:
            @pl.loop(0, n)
            def _(c):
                step(k_ref, ve_ref, vo_ref, pl.ds(pl.multiple_of(c * tk, tk), tk), tk)

    first = lax.broadcasted_iota(jnp.int32, (tq, LANES), 1) < MLA_V
    for pj in range(H // 2):
        a0, a1 = acc_scr[2 * pj], acc_scr[2 * pj + 1]
        out = jnp.where(first, a0 / a0[:, MLA_V:MLA_V + 1], a1 / a1[:, 0:1])
        o_ref[0, :, pj * LANES:(pj + 1) * LANES] = out.astype(BF16)


def _attention(q, sources):
    B, L, _ = q.shape
    tq = min(ATT_TQ, L)
    lengths =tuple(s[0].shape[1] for s in sources)
    in_specs = [pl.BlockSpec((1, tq, MQ_W), lambda b, i: (b, i, 0))]
    args = [q]
    for k, ve, vo in sources:
        S = k.shape[1]
        in_specs += [pl.BlockSpec((1, S, MQ_W), lambda b, i: (b, 0, 0)),
                     pl.BlockSpec((1, S, 512), lambda b, i: (b, 0, 0)),
                     pl.BlockSpec((1, S, 512), lambda b, i: (b, 0, 0))]
        args += [k, ve, vo]
    return pl.pallas_call(
        functools.partial(_attn_kernel, lengths=lengths),
        out_shape=jax.ShapeDtypeStruct((B, L, MLA_HEADS * MLA_V), BF16),
        grid=(B, L // tq),
        in_specs=in_specs,
        out_specs=pl.BlockSpec((1, tq, MLA_HEADS * MLA_V), lambda b, i: (b, i, 0)),
        scratch_shapes=[pltpu.VMEM((ATT_SLOTS, tq, ATT_TK), F32),
                        pltpu.VMEM((2, tq, ATT_TK), BF16),
                        pltpu.VMEM((MLA_HEADS, tq, LANES), F32),
                        pltpu.VMEM((2, tq, LANES), F32),
                        pltpu.VMEM((MLA_HEADS, tq, LANES), F32)],
        compiler_params=_params("arbitrary", "arbitrary"),
        name="attention",
    )(*args)


def _mixer_kernel(a_ref, ap_ref, an_ref, cw_ref, cb_ref, lg_ref, lb_ref, yr_ref, at_ref, wo_ref,
                  pg_ref, gate_ref, x_ref, o_ref, win_scr, shf_scr, yc_scr):
    tm = a_ref.shape[1]
    i = pl.program_id(1)
    n = pl.num_programs(1)
    win_scr[0:HALO] = jnp.where(i > 0, ap_ref[0], 0.0)
    win_scr[HALO:HALO + tm] = a_ref[0]
    win_scr[HALO + tm:HALO + tm + HALO] = jnp.where(i < n - 1, an_ref[0], 0.0)
    span = tm + CONV_SPAN
    for ph in range(1, SUBLANES):
        shf_scr[ph - 1] = win_scr[ph:ph + span, :]
    sub = 64
    off = HALO - CONV_K // 2
    for r0 in range(0, tm, sub):
        acc = jnp.zeros((sub, CONV_CH), F32) + cb_ref[...]
        for t in range(CONV_K):
            ph = (off + t) % SUBLANES
            base = r0 + off + t - ph
            rows = win_scr[base:base + sub, :] if ph == 0 else shf_scr[ph - 1, base:base + sub, :]
            acc = acc + cw_ref[t:t + 1, :] * rows
        mu = jnp.mean(acc, axis=-1, keepdims=True)
        d = acc - mu
        var = jnp.mean(d * d, axis=-1, keepdims=True)
        yc = d * lax.rsqrt(var + EPS) * lg_ref[...] + lb_ref[...]
        yc_scr[r0:r0 + sub, :] = _silu(yc).astype(BF16)
    y = jnp.dot(yc_scr[...], wo_ref[0:256, :], preferred_element_type=F32)
    y = y + jnp.dot(yr_ref[0], wo_ref[256:512, :], preferred_element_type=F32)
    y = y + jnp.dot(at_ref[0], wo_ref[512:1024, :], preferred_element_type=F32)
    r = y * lax.rsqrt(jnp.mean(y * y, axis=-1, keepdims=True) + EPS) * pg_ref[...]
    o_ref[0] = x_ref[0] + gate_ref[0] * r


def _mixer(a, cw, cb, lg, lb, yr, att, wo, pg, mod, x):
    B, L, D = x.shape
    tm = min(ROW_TILE, L)
    hb = tm // HALO
    nh = L // HALO
    row = lambda w: pl.BlockSpec((1, tm, w), lambda b, i: (b, i, 0))
    return pl.pallas_call(
        _mixer_kernel,
        out_shape=jax.ShapeDtypeStruct((B, L, D), F32),
        grid=(B, L // tm),
        in_specs=[row(CONV_CH),
                  pl.BlockSpec((1, HALO, CONV_CH), lambda b, i: (b, jnp.maximum(i * hb - 1, 0), 0)),
                  pl.BlockSpec((1, HALO, CONV_CH), lambda b, i: (b, jnp.minimum((i + 1) * hb, nh - 1), 0)),
                  cw.spec, cb.spec, lg.spec, lb.spec, row(256), row(512), wo.spec, pg.spec, mod.spec(2), row(D)],
        out_specs=row(D),
        scratch_shapes=[pltpu.VMEM((tm + 2 * HALO, CONV_CH), F32),
                        pltpu.VMEM((SUBLANES - 1, tm + CONV_SPAN, CONV_CH), F32),
                        pltpu.VMEM((tm, CONV_CH), BF16)],
        compiler_params=_params("arbitrary", "arbitrary"),
        name="mixer_out",
    )(a, a, a, cw.arr, cb.arr, lg.arr, lb.arr, yr, att, wo.arr, pg.arr, mod.table, x)


def _ffn_kernel(x_ref, sh_ref, sc_ref, g_ref, w1_ref, w2_ref, pg_ref, gate_ref, o_ref):
    x = x_ref[0]
    h = x * lax.rsqrt(jnp.mean(x * x, axis=-1, keepdims=True) + EPS) * g_ref[...]
    h = (h * (1.0 + sc_ref[0]) + sh_ref[0]).astype(BF16)
    y = jnp.zeros(x.shape, F32)
    for lo, hi in FF_CHUNKS:
        u = jnp.dot(h, w1_ref[:, lo:hi], preferred_element_type=F32)
        gt = jnp.dot(h, w1_ref[:, D_FF + lo:D_FF + hi], preferred_element_type=F32)
        y = y + jnp.dot((_silu(gt) * u).astype(BF16), w2_ref[lo:hi, :], preferred_element_type=F32)
    r = y * lax.rsqrt(jnp.mean(y * y, axis=-1, keepdims=True) + EPS) * pg_ref[...]
    o_ref[0] = x + gate_ref[0] * r


def _ffn(x, mod, g, w1, w2, pg):
    B, L, D = x.shape
    tm = min(FFN_TILE, L)
    row = pl.BlockSpec((1, tm, D), lambda b, i: (b, i, 0))
    return pl.pallas_call(
        _ffn_kernel,
        out_shape=jax.ShapeDtypeStruct((B, L, D), F32),
        grid=(B, L // tm),
        in_specs=[row, mod.spec(3), mod.spec(4), g.spec, w1.spec, w2.spec, pg.spec, mod.spec(5)],
        out_specs=row,
        compiler_params=_params("arbitrary", "arbitrary"),
        name="ffn",
    )(x, mod.table, mod.table, g.arr, w1.arr, w2.arr, pg.arr, mod.table)


def _rope_tables(T, rotary):
    t = np.arange(T)
    rowp = (t // GRID_W).astype(np.float64)[:, None]
    colp = (t % GRID_W).astype(np.float64)[:, None]

    def half(f):
        inv = ROPE_BASE ** (-np.arange(f, dtype=np.float64) / f)
        ar, ac = rowp * inv[None, :], colp * inv[None, :]
        cos = np.concatenate([np.cos(ar)] * 2 + [np.cos(ac)] * 2, axis=-1)
        sin_r, sin_c, z = np.sin(ar), np.sin(ac), np.zeros_like(ar)
        sin_up = np.concatenate([-sin_r, z, -sin_c, z], axis=-1)
        sin_dn = np.concatenate([z, sin_r, z, sin_c], axis=-1)
        if not rotary:
            cos, sin_up, sin_dn = np.ones_like(cos), np.zeros_like(cos), np.zeros_like(cos)
        return cos, sin_up, sin_dn

    ret = [np.tile(a, (1, 2)) for a in half(RET_DK // 4)]

    def pad(a, fill):
        return np.concatenate([np.full((T, MLA_NOPE), fill), a,
                               np.full((T, LANES - MLA_NOPE - MLA_ROPE), fill)], axis=-1)

    cm, su, sd = half(MLA_ROPE // 4)
    return jnp.asarray(np.stack(ret + [pad(cm, 1.0), pad(su, 0.0), pad(sd, 0.0)]).astype(np.float32))


def _stacked_weights(w_in, mla_w_uq, mla_w_ukv, w_out, ffn_w_in, ffn_w_out):
    L, D, _ = w_in.shape
    offs = [0]
    for s in (2 * CONV_CH, 256, 256, 256, 256, MLA_Q_RANK, MLA_KV_RANK, MLA_ROPE):
        offs.append(offs[-1] + s)
    wa, wq, wk, wv, wg, wcq, wckv, wkr = [w_in[:, :, offs[i]:offs[i + 1]] for i in range(8)]
    zl = jnp.zeros((L, D, MLA_NOPE), F32)
    zr = jnp.zeros((L, D, LANES - MLA_NOPE - MLA_ROPE), F32)
    w_ext = jnp.concatenate([wa, wq, wk * (RET_DK ** -0.5), wv, wg, wcq, wckv, zl, wkr, zr], axis=2).astype(BF16)

    uq = mla_w_uq.reshape(L, MLA_Q_RANK, MLA_HEADS, MLA_NOPE + MLA_ROPE)
    zq = jnp.zeros((L, MLA_Q_RANK, MLA_HEADS, LANES - MLA_NOPE - MLA_ROPE), F32)
    wuq = jnp.concatenate([uq, zq], axis=-1).reshape(L, MLA_Q_RANK, MQ_W).astype(BF16)

    ukv = mla_w_ukv.reshape(L, MLA_KV_RANK, MLA_HEADS, MLA_NOPE + MLA_V)
    zk = jnp.zeros((L, MLA_KV_RANK, MLA_HEADS, LANES - MLA_NOPE), F32)
    uk_pad = jnp.concatenate([ukv[..., :MLA_NOPE], zk], axis=-1).reshape(L, MLA_KV_RANK, MQ_W)
    uv = ukv[..., MLA_NOPE:].reshape(L, MLA_KV_RANK, MLA_HEADS * MLA_V)
    wukv = jnp.concatenate([uk_pad, uv], axis=2).astype(BF16)
    return w_ext, wuq, wukv, w_out.astype(BF16), ffn_w_in.astype(BF16), ffn_w_out.astype(BF16)


def _pack_vectors(*vectors):
    packed = jnp.concatenate(vectors, axis=-1)[:, None, :]
    blocks, off = [], 0
    for v in vectors:
        w = v.shape[-1]
        assert off % w == 0
        blocks.append((off // w, w))
        off += w
    return lambda layer: [_LayerLanes(packed, layer, blk, w) for blk, w in blocks]


def kernel(x, c, ctx, c_ctx, mod_w, mod_b, pre1_g, post1_g, pre2_g, post2_g, w_in, conv_w, conv_b,
           conv_ln_g, conv_ln_b, ret_log_decay, ret_gn_g, mla_q_norm_g, mla_w_uq, mla_kv_norm_g,
           mla_w_ukv, w_out, ffn_w_in, ffn_w_out):
    B, T, D = x.shape
    TC = ctx.shape[1]
    cv = jnp.concatenate([c, c_ctx[None], jnp.zeros((MOD_ROWS - B - 1, D), F32)], axis=0)
    mod_table = _modulation(cv, mod_w, mod_b).reshape(DEPTH * MOD_ROWS * 6, 1, D)

    tabs_lat = _rope_tables(T, True)
    tabs_ctx = _rope_tables(TC, False)
    zstate = jnp.zeros((B, RET_HEADS // 2, LANES, LANES), F32)

    stacks = _stacked_weights(w_in, mla_w_uq, mla_w_ukv, w_out, ffn_w_in, ffn_w_out)
    stacks += (jnp.pad(conv_w, ((0, 0), (0, 1), (0, 0))),)
    lg_lanes = jnp.repeat(ret_log_decay, RET_DK, axis=-1)
    vectors = _pack_vectors(pre1_g, post1_g, pre2_g, post2_g, mla_q_norm_g, ret_gn_g, lg_lanes[:, 0],
                            lg_lanes[:, 1], conv_b, conv_ln_g, conv_ln_b, mla_kv_norm_g)

    xc = ctx
    for l in range(DEPTH):
        last = l == DEPTH - 1
        w_ext, wuq, wukv, wo, w1, w2, cw = [_Layered(a, l) for a in stacks]
        pre1, post1, pre2, post2, qng, gng, lgf, lgb, cb, clg, clb, kvg = vectors(l)
        lat = _ModRows(mod_table, l * MOD_ROWS * 6, 6)
        cx = _ModRows(mod_table, (l * MOD_ROWS + B) * 6, 0)
        lgd = ret_log_decay[l]

        proj = lambda xx, md, tabs: _inproj(xx, md, pre1, w_ext, qng, wuq, kvg, wukv, tabs)
        aL, rqL, rkL, rvL, rgL, mqL, mkL, mveL, mvoL = proj(x, lat, tabs_lat)
        aC, rqC, rkC, rvC, rgC, mqC, mkC, mveC, mvoC = proj(xc, cx, tabs_ctx)

        yrC, sf, sb = _retention(lgd, rqC, rkC, rvC, rgC, gng, lgf, lgb, zstate, zstate)
        yrL, _, _ = _retention(lgd, rqL, rkL, rvL, rgL, gng, lgf, lgb, sf, sb)

        attL = _attention(mqL, [(mkC, mveC, mvoC), (mkL, mveL, mvoL)])
        mix = lambda a, yr, att, md, xx: _mixer(a, cw, cb, clg, clb, yr, att, wo, post1, md, xx)
        ffn = lambda xx, md: _ffn(xx, md, pre2, w1, w2, post2)
        x = mix(aL, yrL, attL, lat, x)
        if not last:
            attC = _attention(mqC, [(mkC, mveC, mvoC)])
            xc = mix(aC, yrC, attC, cx, xc)
        x = ffn(x, lat)
        if not last:
            xc = ffn(xc, cx)
    return x
```

```python
import functools

import numpy as np
import jax
import jax.numpy as jnp
from jax import lax
from jax.experimental import pallas as pl
from jax.experimental.pallas import tpu as pltpu

F32 = jnp.float32
BF16 = jnp.bfloat16

D_MODEL = 1024
DEPTH = 2
GRID_W = 64
EPS = 1e-6
LOG2E = 1.4426950408889634
ROPE_BASE = 10000.0
CONV_CH = 256
CONV_K = 31
RET_HEADS = 4
RET_DK = 64
RET_DV = 64
MLA_HEADS = 8
MLA_Q_RANK = 256
MLA_KV_RANK = 128
MLA_NOPE = 64
MLA_ROPE = 32
MLA_V = 64
D_FF = 2816

LANES = 128
SUBLANES = 8
HALO = 16
CONV_SPAN = 24
ROW_TILE = 512
FFN_TILE = 1024
FFN_SUB = 512
RET_CHUNK = 256
ATT_TQ = 512
ATT_TK = 512
ATT_AHEAD = 2
ATT_SLOTS = ATT_AHEAD + 1
ATT_ROWS = 32
MXU_DIM = 256
FF_CHUNKS = ((0, 6 * MXU_DIM), (6 * MXU_DIM, D_FF))
VMEM_LIMIT = 56 * 1024 * 1024

_OFF_A = 0
_OFF_Q = 512
_OFF_K = 768
_OFF_V = 1024
_OFF_G = 1280
_OFF_CQ = 1536
_OFF_CKV = 1792
_OFF_KRP = 1920
D_EXT = 2048
MOD_ROWS = 8
MQ_W = MLA_HEADS * LANES


def _sigmoid(x):
    return 1.0 / (1.0 + jnp.exp(-x))


def _silu(x):
    return x * _sigmoid(x)


def _params(*sem):
    return pltpu.CompilerParams(dimension_semantics=sem, vmem_limit_bytes=VMEM_LIMIT)


class _Layered:
    def __init__(self, arr, layer):
        self.arr, self.layer = arr, layer

    @property
    def spec(self):
        n, layer = self.arr.ndim - 1, self.layer
        return pl.BlockSpec((None,) + self.arr.shape[1:], lambda *_: (layer,) + (0,) * n,
                            pipeline_mode=pl.Buffered(1))


class _LayerLanes:
    def __init__(self, arr, layer, block, width):
        self.arr, self.layer, self.block, self.width = arr, layer, block, width

    @property
    def spec(self):
        layer, block = self.layer, self.block
        return pl.BlockSpec((None, 1, self.width), lambda *_: (layer, 0, block), pipeline_mode=pl.Buffered(1))


class _ModRows:
    def __init__(self, table, base, per_batch):
        self.table, self.base, self.per_batch = table, base, per_batch

    def spec(self, j):
        base, step = self.base + j, self.per_batch
        return pl.BlockSpec((1, 1, self.table.shape[-1]), lambda b, i: (base + step * b, 0, 0))


def _mod_kernel(cv_ref, w_ref, b_ref, o_ref):
    s = _silu(cv_ref[...])
    o_ref[0] = jnp.dot(s.astype(BF16), w_ref[0].astype(BF16), preferred_element_type=F32) + b_ref[0]


def _modulation(cv, mod_w, mod_b):
    L, D, N = mod_w.shape
    tn = 1536
    return pl.pallas_call(
        _mod_kernel,
        out_shape=jax.ShapeDtypeStruct((L, 8, N), F32),
        grid=(L, N // tn),
        in_specs=[pl.BlockSpec((8, D), lambda l, j: (0, 0)),
                  pl.BlockSpec((1, D, tn), lambda l, j: (l, 0, j)),
                  pl.BlockSpec((1, 1, tn), lambda l, j: (l, 0, j))],
        out_specs=pl.BlockSpec((1, 8, tn), lambda l, j: (l, 0, j)),
        compiler_params=_params("arbitrary", "arbitrary"),
        name="modulation",
    )(cv, mod_w, mod_b.reshape(L, 1, N))


def _rope(x, cos, sin_up, sin_dn, f):
    return x * cos + pltpu.roll(x, LANES - f, 1) * sin_up + pltpu.roll(x, f, 1) * sin_dn


def _inproj_kernel(x_ref, sh_ref, sc_ref, g_ref, w_ref, qng_ref, wuq_ref, kvg_ref, wukv_ref, tab_ref,
                   a_ref, rq_ref, rk_ref, rv_ref, rg_ref, mq_ref, mk_ref, mve_ref, mvo_ref):
    x = x_ref[0]
    h = x * lax.rsqrt(jnp.mean(x * x, axis=-1, keepdims=True) + EPS) * g_ref[...]
    hb = (h * (1.0 + sc_ref[0]) + sh_ref[0]).astype(BF16)

    def proj(lo, hi):
        return jnp.dot(hb, w_ref[:, lo:hi], preferred_element_type=F32)

    pm = proj(_OFF_CQ, D_EXT)
    pr = proj(_OFF_Q, _OFF_V)
    cq = pm[:, 0:MLA_Q_RANK]
    qn = cq * lax.rsqrt(jnp.mean(cq * cq, axis=-1, keepdims=True) + EPS) * qng_ref[...]
    qq = jnp.dot(qn.astype(BF16), wuq_ref[...], preferred_element_type=F32)
    ckv = pm[:, _OFF_CKV - _OFF_CQ:_OFF_KRP - _OFF_CQ]
    kvn = ckv * lax.rsqrt(jnp.mean(ckv * ckv, axis=-1, keepdims=True) + EPS) * kvg_ref[...]
    kv = jnp.dot(kvn.astype(BF16), wukv_ref[...], preferred_element_type=F32)
    pa = proj(_OFF_A, _OFF_Q)
    pv = proj(_OFF_V, _OFF_CQ)

    a_ref[0] = pa[:, 0:CONV_CH] * _sigmoid(pa[:, CONV_CH:2 * CONV_CH])

    ret_tabs = (tab_ref[0], tab_ref[1], tab_ref[2], RET_DK // 4)
    for i in range(2):
        lo, hi = i * LANES, (i + 1) * LANES
        rq_ref[0, :, lo:hi] = _rope(pr[:, lo:hi], *ret_tabs).astype(BF16)
        rk_ref[0, :, lo:hi] = _rope(pr[:, 256 + lo:256 + hi], *ret_tabs).astype(BF16)
    rv_ref[0] = pv[:, 0:256].astype(BF16)
    rg_ref[0] = pv[:, 256:512]

    mla_tabs = (tab_ref[3], tab_ref[4], tab_ref[5], MLA_ROPE // 4)
    scale = float((MLA_NOPE + MLA_ROPE) ** -0.5 * LOG2E)
    kr_slot = _rope(pm[:, _OFF_KRP - _OFF_CQ:D_EXT - _OFF_CQ], *mla_tabs)
    for hh in range(MLA_HEADS):
        lo, hi = hh * LANES, (hh + 1) * LANES
        mq_ref[0, :, lo:hi] = (_rope(qq[:, lo:hi], *mla_tabs) * scale).astype(BF16)
        mk_ref[0, :, lo:hi] = (kv[:, lo:hi] + kr_slot).astype(BF16)
    v = kv[:, MQ_W:]
    lane = lax.broadcasted_iota(jnp.int32, v.shape, 1)
    lane = lane % LANES
    even = lane < MLA_V
    mve_ref[0] = jnp.where(even, v, jnp.where(lane == MLA_V, 1.0, 0.0)).astype(BF16)
    mvo_ref[0] = jnp.where(even, jnp.where(lane == 0, 1.0, 0.0), v).astype(BF16)


def _inproj(x, mod, g, w_ext, qng, wuq, kvg, wukv, tabs):
    B, L, D = x.shape
    tm = min(ROW_TILE, L)
    row = lambda w: pl.BlockSpec((1, tm, w), lambda b, i: (b, i, 0))
    outs = [(CONV_CH, F32), (256, BF16), (256, BF16), (256, BF16), (256, F32),
            (MQ_W, BF16), (MQ_W, BF16), (512, BF16), (512, BF16)]
    layered = [g, w_ext, qng, wuq, kvg, wukv]
    return pl.pallas_call(
        _inproj_kernel,
        out_shape=[jax.ShapeDtypeStruct((B, L, w), dt) for w, dt in outs],
        grid=(B, L // tm),
        in_specs=[row(D), mod.spec(0), mod.spec(1)] + [a.spec for a in layered]
        + [pl.BlockSpec((tabs.shape[0], tm, LANES), lambda b, i: (0, i, 0))],
        out_specs=[row(w) for w, _ in outs],
        compiler_params=_params("arbitrary", "arbitrary"),
        name="inproj",
    )(x, mod.table, mod.table, *[a.arr for a in layered], tabs)


def _ret_kernel(lgd_ref, q_ref, k_ref, v_ref, g_ref, gng_ref, lgf_ref, lgb_ref, sf0_ref, sb0_ref,
                y_ref, sfo_ref, sbo_ref, w_scr, dec_scr, kvf_scr, kvb_scr, sfs_scr, sbs_scr, *, n_chunks):
    C = RET_CHUNK
    N = n_chunks
    P = RET_HEADS // 2
    U = 4 if N % 4 == 0 else 1

    @pl.when(pl.program_id(0) == 0)
    def _():
        t = lax.broadcasted_iota(jnp.int32, (C, C), 0)
        m = lax.broadcasted_iota(jnp.int32, (C, C), 1)
        d = (t - m).astype(F32)
        for hh in range(RET_HEADS):
            w_scr[hh] = jnp.exp(jnp.where(d >= 0.0, lgd_ref[0, hh] * d, lgd_ref[1, hh] * (-d)))
        j = lax.broadcasted_iota(jnp.int32, (C, LANES), 0).astype(F32)
        for pi in range(P):
            lgf = lgf_ref[:, pi * LANES:(pi + 1) * LANES]
            lgb = lgb_ref[:, pi * LANES:(pi + 1) * LANES]
            dec_scr[pi, 0] = jnp.exp(lgf * (C - 1.0 - j))
            dec_scr[pi, 1] = jnp.exp(lgb * j)
            dec_scr[pi, 2] = jnp.exp(lgf * (j + 1.0))
            dec_scr[pi, 3] = jnp.exp(lgb * (C - j))

    first = lax.broadcasted_iota(jnp.int32, (C, LANES), 1) < RET_DV
    r128 = lax.broadcasted_iota(jnp.int32, (LANES, LANES), 0)
    c128 = lax.broadcasted_iota(jnp.int32, (LANES, LANES), 1)
    blockdiag = (r128 < RET_DK) == (c128 < RET_DV)
    dn = (((1,), (1,)), ((), ()))

    def sums_one(i, pi):
        rows = pl.ds(pl.multiple_of(i * C, C), C)
        lo, hi = pi * LANES, (pi + 1) * LANES
        kf = k_ref[0, rows, lo:hi].astype(F32)
        vv = v_ref[0, rows, lo:hi]
        kzf = (kf * dec_scr[pi, 0]).T.astype(BF16)
        kzb = (kf * dec_scr[pi, 1]).T.astype(BF16)
        kvf_scr[pi, i] = jnp.where(blockdiag, jnp.dot(kzf, vv, preferred_element_type=F32), 0.0)
        kvb_scr[pi, i] = jnp.where(blockdiag, jnp.dot(kzb, vv, preferred_element_type=F32), 0.0)

    def out_one(i, pi):
        rows = pl.ds(pl.multiple_of(i * C, C), C)
        lo, hi = pi * LANES, (pi + 1) * LANES
        q = q_ref[0, rows, lo:hi]
        k = k_ref[0, rows, lo:hi]
        vv = v_ref[0, rows, lo:hi]
        zero = jnp.zeros_like(q)
        s0 = lax.dot_general(jnp.where(first, q, zero), k, dn, preferred_element_type=F32)
        s1 = lax.dot_general(jnp.where(first, zero, q), k, dn, preferred_element_type=F32)
        p0 = (s0 * w_scr[2 * pi]).astype(BF16)
        p1 = (s1 * w_scr[2 * pi + 1]).astype(BF16)
        o = jnp.dot(p0, jnp.where(first, vv, zero), preferred_element_type=F32)
        o = o + jnp.dot(p1, jnp.where(first, zero, vv), preferred_element_type=F32)
        o = o + jnp.dot(q, sfs_scr[pi, i].astype(BF16), preferred_element_type=F32) * dec_scr[pi, 2]
        o = o + jnp.dot(q, sbs_scr[pi, i].astype(BF16), preferred_element_type=F32) * dec_scr[pi, 3]
        inv = 1.0 / RET_DV
        tot = jnp.sum(o, axis=-1, keepdims=True)
        m0 = jnp.sum(jnp.where(first, o, 0.0), axis=-1, keepdims=True)
        mu = jnp.where(first, m0, tot - m0) * inv
        dlt = o - mu
        d2 = dlt * dlt
        tot2 = jnp.sum(d2, axis=-1, keepdims=True)
        v0 = jnp.sum(jnp.where(first, d2, 0.0), axis=-1, keepdims=True)
        var = jnp.where(first, v0, tot2 - v0) * inv
        y = dlt * lax.rsqrt(var + EPS) * gng_ref[:, lo:hi]
        y_ref[0, rows, lo:hi] = (y * _silu(g_ref[0, rows, lo:hi])).astype(BF16)

    def over_chunks(fn):
        def body(t, carry):
            for u in range(U):
                for pi in range(P):
                    fn(t * U + u, pi)
            return carry
        lax.fori_loop(0, N // U, body, 0)

    over_chunks(sums_one)

    gc_f = [jnp.exp(lgf_ref[:, pi * LANES:(pi + 1) * LANES] * float(C)) for pi in range(P)]
    gc_b = [jnp.exp(lgb_ref[:, pi * LANES:(pi + 1) * LANES] * float(C)) for pi in range(P)]

    def fwd_scan(i, ss):
        for pi in range(P):
            sfs_scr[pi, i] = ss[pi]
        return tuple(ss[pi] * gc_f[pi] + kvf_scr[pi, i] for pi in range(P))

    def bwd_scan(ii, ss):
        i = N - 1 - ii
        for pi in range(P):
            sbs_scr[pi, i] = ss[pi]
        return tuple(ss[pi] * gc_b[pi] + kvb_scr[pi, i] for pi in range(P))

    sf = lax.fori_loop(0, N, fwd_scan, tuple(sf0_ref[0, pi] for pi in range(P)))
    sb = lax.fori_loop(0, N, bwd_scan, tuple(sb0_ref[0, pi] for pi in range(P)))
    for pi in range(P):
        sfo_ref[0, pi] = sf[pi]
        sbo_ref[0, pi] = sb[pi]

    over_chunks(out_one)


def _retention(lgd, q, k, v, g, gng, lgf, lgb, sf0, sb0):
    B, L, W = q.shape
    C = RET_CHUNK
    N = L // C
    P = RET_HEADS // 2
    seq = lambda: pl.BlockSpec((1, L, W), lambda b: (b, 0, 0))
    st = lambda: pl.BlockSpec((1, P, LANES, LANES), lambda b: (b, 0, 0, 0))
    st_shape = jax.ShapeDtypeStruct((B, P, LANES, LANES), F32)
    return pl.pallas_call(
        functools.partial(_ret_kernel, n_chunks=N),
        out_shape=[jax.ShapeDtypeStruct((B, L, W), BF16), st_shape, st_shape],
        grid=(B,),
        in_specs=[pl.BlockSpec(memory_space=pltpu.SMEM), seq(), seq(), seq(), seq(),
                  gng.spec, lgf.spec, lgb.spec, st(), st()],
        out_specs=[seq(), st(), st()],
        scratch_shapes=[pltpu.VMEM((RET_HEADS, C, C), F32), pltpu.VMEM((P, 4, C, LANES), F32)]
        + [pltpu.VMEM((P, N, LANES, LANES), F32) for _ in range(4)],
        compiler_params=_params("arbitrary"),
        name="retention",
    )(lgd, q, k, v, g, gng.arr, lgf.arr, lgb.arr, sf0, sb0)


def _attn_kernel(*refs, lengths):
    q_ref = refs[0]
    src = refs[1:1 + 3 * len(lengths)]
    o_ref, s_scr, p_scr, m_scr, a_scr, acc_scr = refs[1 + 3 * len(lengths):]
    tq = q_ref.shape[1]
    dn = (((1,), (1,)), ((), ()))
    H = MLA_HEADS
    RB = ATT_ROWS

    m_scr[...] = jnp.full(m_scr.shape, -1e30, F32)
    acc_scr[...] = jnp.zeros(acc_scr.shape, F32)

    def step(k_ref, ve_ref, vo_ref, rows, tk):
        def scores(h):
            q = q_ref[0, :, h * LANES:(h + 1) * LANES]
            k = k_ref[0, rows, h * LANES:(h + 1) * LANES]
            s_scr[h % ATT_SLOTS, :, 0:tk] = lax.dot_general(q, k, dn, preferred_element_type=F32)

        for h in range(ATT_AHEAD):
            scores(h)
        for h in range(H):
            if h + ATT_AHEAD < H:
                scores(h + ATT_AHEAD)
            for r0 in range(0, tq, RB):
                s = s_scr[h % ATT_SLOTS, r0:r0 + RB, 0:tk]
                m_old = m_scr[h, r0:r0 + RB, :]
                n = jnp.maximum(m_old, jnp.max(s, axis=-1, keepdims=True))
                p_scr[h % 2, r0:r0 + RB, 0:tk] = jnp.exp2(s - jnp.tile(n, (1, tk // LANES))).astype(BF16)
                a_scr[h % 2, r0:r0 + RB, :] = jnp.exp2(m_old - n)
                m_scr[h, r0:r0 + RB, :] = n
            v_ref = ve_ref if h % 2 == 0 else vo_ref
            v = v_ref[0, rows, (h // 2) * LANES:(h // 2 + 1) * LANES]
            acc_scr[h] = acc_scr[h] * a_scr[h % 2] + jnp.dot(p_scr[h % 2, :, 0:tk], v, preferred_element_type=F32)

    for si, length in enumerate(lengths):
        k_ref, ve_ref, vo_ref = src[3 * si:3 * si + 3]
        tk = min(ATT_TK, length)
        n = length // tk
        if n == 1:
            step(k_ref, ve_ref, vo_ref, pl.ds(0, tk), tk)
        else:
            @pl.loop(0, n)
            def _(c):
                step(k_ref, ve_ref, vo_ref, pl.ds(pl.multiple_of(c * tk, tk), tk), tk)

    first = lax.broadcasted_iota(jnp.int32, (tq, LANES), 1) < MLA_V
    for pj in range(H // 2):
        a0, a1 = acc_scr[2 * pj], acc_scr[2 * pj + 1]
        out = jnp.where(first, a0 / a0[:, MLA_V:MLA_V + 1], a1 / a1[:, 0:1])
        o_ref[0, :, pj * LANES:(pj + 1) * LANES] = out.astype(BF16)


def _attention(q, sources):
    B, L, _ = q.shape
    tq = min(ATT_TQ, L)
    lengths =tuple(s[0].shape[1] for s in sources)
    in_specs = [pl.BlockSpec((1, tq, MQ_W), lambda b, i: (b, i, 0))]
    args = [q]
    for k, ve, vo in sources:
        S = k.shape[1]
        in_specs += [pl.BlockSpec((1, S, MQ_W), lambda b, i: (b, 0, 0)),
                     pl.BlockSpec((1, S, 512), lambda b, i: (b, 0, 0)),
                     pl.BlockSpec((1, S, 512), lambda b, i: (b, 0, 0))]
        args += [k, ve, vo]
    return pl.pallas_call(
        functools.partial(_attn_kernel, lengths=lengths),
        out_shape=jax.ShapeDtypeStruct((B, L, MLA_HEADS * MLA_V), BF16),
        grid=(B, L // tq),
        in_specs=in_specs,
        out_specs=pl.BlockSpec((1, tq, MLA_HEADS * MLA_V), lambda b, i: (b, i, 0)),
        scratch_shapes=[pltpu.VMEM((ATT_SLOTS, tq, ATT_TK), F32),
                        pltpu.VMEM((2, tq, ATT_TK), BF16),
                        pltpu.VMEM((MLA_HEADS, tq, LANES), F32),
                        pltpu.VMEM((2, tq, LANES), F32),
                        pltpu.VMEM((MLA_HEADS, tq, LANES), F32)],
        compiler_params=_params("arbitrary", "arbitrary"),
        name="attention",
    )(*args)


def _mixer_kernel(a_ref, ap_ref, an_ref, cw_ref, cb_ref, lg_ref, lb_ref, yr_ref, at_ref, wo_ref,
                  pg_ref, gate_ref, x_ref, o_ref, win_scr, shf_scr, yc_scr):
    tm = a_ref.shape[1]
    i = pl.program_id(1)
    n = pl.num_programs(1)
    win_scr[0:HALO] = jnp.where(i > 0, ap_ref[0], 0.0)
    win_scr[HALO:HALO + tm] = a_ref[0]
    win_scr[HALO + tm:HALO + tm + HALO] = jnp.where(i < n - 1, an_ref[0], 0.0)
    span = tm + CONV_SPAN
    for ph in range(1, SUBLANES):
        shf_scr[ph - 1] = win_scr[ph:ph + span, :]
    sub = 128
    off = HALO - CONV_K // 2
    for r0 in range(0, tm, sub):
        acc = jnp.zeros((sub, CONV_CH), F32) + cb_ref[...]
        for t in range(CONV_K):
            ph = (off + t) % SUBLANES
            base = r0 + off + t - ph
            rows = win_scr[base:base + sub, :] if ph == 0 else shf_scr[ph - 1, base:base + sub, :]
            acc = acc + cw_ref[t:t + 1, :] * rows
        mu = jnp.mean(acc, axis=-1, keepdims=True)
        d = acc - mu
        var = jnp.mean(d * d, axis=-1, keepdims=True)
        yc = d * lax.rsqrt(var + EPS) * lg_ref[...] + lb_ref[...]
        yc_scr[r0:r0 + sub, :] = _silu(yc).astype(BF16)
    y = jnp.dot(yc_scr[...], wo_ref[0:256, :], preferred_element_type=F32)
    y = y + jnp.dot(yr_ref[0], wo_ref[256:512, :], preferred_element_type=F32)
    y = y + jnp.dot(at_ref[0], wo_ref[512:1024, :], preferred_element_type=F32)
    r = y * lax.rsqrt(jnp.mean(y * y, axis=-1, keepdims=True) + EPS) * pg_ref[...]
    o_ref[0] = x_ref[0] + gate_ref[0] * r


def _mixer(a, cw, cb, lg, lb, yr, att, wo, pg, mod, x):
    B, L, D = x.shape
    tm = min(ROW_TILE, L)
    hb = tm // HALO
    nh = L // HALO
    row = lambda w: pl.BlockSpec((1, tm, w), lambda b, i: (b, i, 0))
    return pl.pallas_call(
        _mixer_kernel,
        out_shape=jax.ShapeDtypeStruct((B, L, D), F32),
        grid=(B, L // tm),
        in_specs=[row(CONV_CH),
                  pl.BlockSpec((1, HALO, CONV_CH), lambda b, i: (b, jnp.maximum(i * hb - 1, 0), 0)),
                  pl.BlockSpec((1, HALO, CONV_CH), lambda b, i: (b, jnp.minimum((i + 1) * hb, nh - 1), 0)),
                  cw.spec, cb.spec, lg.spec, lb.spec, row(256), row(512), wo.spec, pg.spec, mod.spec(2), row(D)],
        out_specs=row(D),
        scratch_shapes=[pltpu.VMEM((tm + 2 * HALO, CONV_CH), F32),
                        pltpu.VMEM((SUBLANES - 1, tm + CONV_SPAN, CONV_CH), F32),
                        pltpu.VMEM((tm, CONV_CH), BF16)],
        compiler_params=_params("arbitrary", "arbitrary"),
        name="mixer_out",
    )(a, a, a, cw.arr, cb.arr, lg.arr, lb.arr, yr, att, wo.arr, pg.arr, mod.table, x)


def _ffn_kernel(x_ref, sh_ref, sc_ref, g_ref, w1_ref, w2_ref, pg_ref, gate_ref, o_ref):
    sub = min(FFN_SUB, x_ref.shape[1])
    for r0 in range(0, x_ref.shape[1], sub):
        x = x_ref[0, r0:r0 + sub, :]
        h = x * lax.rsqrt(jnp.mean(x * x, axis=-1, keepdims=True) + EPS) * g_ref[...]
        h = (h * (1.0 + sc_ref[0]) + sh_ref[0]).astype(BF16)
        y = jnp.zeros(x.shape, F32)
        for lo, hi in FF_CHUNKS:
            u = jnp.dot(h, w1_ref[:, lo:hi], preferred_element_type=F32)
            gt = jnp.dot(h, w1_ref[:, D_FF + lo:D_FF + hi], preferred_element_type=F32)
            y = y + jnp.dot((_silu(gt) * u).astype(BF16), w2_ref[lo:hi, :], preferred_element_type=F32)
        r = y * lax.rsqrt(jnp.mean(y * y, axis=-1, keepdims=True) + EPS) * pg_ref[...]
        o_ref[0, r0:r0 + sub, :] = x + gate_ref[0] * r


def _ffn(x, mod, g, w1, w2, pg):
    B, L, D = x.shape
    tm = min(FFN_TILE, L)
    row = pl.BlockSpec((1, tm, D), lambda b, i: (b, i, 0))
    return pl.pallas_call(
        _ffn_kernel,
        out_shape=jax.ShapeDtypeStruct((B, L, D), F32),
        grid=(B, L // tm),
        in_specs=[row, mod.spec(3), mod.spec(4), g.spec, w1.spec, w2.spec, pg.spec, mod.spec(5)],
        out_specs=row,
        compiler_params=_params("arbitrary", "arbitrary"),
        name="ffn",
    )(x, mod.table, mod.table, g.arr, w1.arr, w2.arr, pg.arr, mod.table)


def _rope_tables(T, rotary):
    t = np.arange(T)
    rowp = (t // GRID_W).astype(np.float64)[:, None]
    colp = (t % GRID_W).astype(np.float64)[:, None]

    def half(f):
        inv = ROPE_BASE ** (-np.arange(f, dtype=np.float64) / f)
        ar, ac = rowp * inv[None, :], colp * inv[None, :]
        cos = np.concatenate([np.cos(ar)] * 2 + [np.cos(ac)] * 2, axis=-1)
        sin_r, sin_c, z = np.sin(ar), np.sin(ac), np.zeros_like(ar)
        sin_up = np.concatenate([-sin_r, z, -sin_c, z], axis=-1)
        sin_dn = np.concatenate([z, sin_r, z, sin_c], axis=-1)
        if not rotary:
            cos, sin_up, sin_dn = np.ones_like(cos), np.zeros_like(cos), np.zeros_like(cos)
        return cos, sin_up, sin_dn

    ret = [np.tile(a, (1, 2)) for a in half(RET_DK // 4)]

    def pad(a, fill):
        return np.concatenate([np.full((T, MLA_NOPE), fill), a,
                               np.full((T, LANES - MLA_NOPE - MLA_ROPE), fill)], axis=-1)

    cm, su, sd = half(MLA_ROPE // 4)
    return jnp.asarray(np.stack(ret + [pad(cm, 1.0), pad(su, 0.0), pad(sd, 0.0)]).astype(np.float32))


def _stacked_weights(w_in, mla_w_uq, mla_w_ukv, w_out, ffn_w_in, ffn_w_out):
    L, D, _ = w_in.shape
    offs = [0]
    for s in (2 * CONV_CH, 256, 256, 256, 256, MLA_Q_RANK, MLA_KV_RANK, MLA_ROPE):
        offs.append(offs[-1] + s)
    wa, wq, wk, wv, wg, wcq, wckv, wkr = [w_in[:, :, offs[i]:offs[i + 1]] for i in range(8)]
    zl = jnp.zeros((L, D, MLA_NOPE), F32)
    zr = jnp.zeros((L, D, LANES - MLA_NOPE - MLA_ROPE), F32)
    w_ext = jnp.concatenate([wa, wq, wk * (RET_DK ** -0.5), wv, wg, wcq, wckv, zl, wkr, zr], axis=2).astype(BF16)

    uq = mla_w_uq.reshape(L, MLA_Q_RANK, MLA_HEADS, MLA_NOPE + MLA_ROPE)
    zq = jnp.zeros((L, MLA_Q_RANK, MLA_HEADS, LANES - MLA_NOPE - MLA_ROPE), F32)
    wuq = jnp.concatenate([uq, zq], axis=-1).reshape(L, MLA_Q_RANK, MQ_W).astype(BF16)

    ukv = mla_w_ukv.reshape(L, MLA_KV_RANK, MLA_HEADS, MLA_NOPE + MLA_V)
    zk = jnp.zeros((L, MLA_KV_RANK, MLA_HEADS, LANES - MLA_NOPE), F32)
    uk_pad = jnp.concatenate([ukv[..., :MLA_NOPE], zk], axis=-1).reshape(L, MLA_KV_RANK, MQ_W)
    uv = ukv[..., MLA_NOPE:].reshape(L, MLA_KV_RANK, MLA_HEADS * MLA_V)
    wukv = jnp.concatenate([uk_pad, uv], axis=2).astype(BF16)
    return w_ext, wuq, wukv, w_out.astype(BF16), ffn_w_in.astype(BF16), ffn_w_out.astype(BF16)


def _pack_vectors(*vectors):
    packed = jnp.concatenate(vectors, axis=-1)[:, None, :]
    blocks, off = [], 0
    for v in vectors:
        w = v.shape[-1]
        assert off % w == 0
        blocks.append((off // w, w))
        off += w
    return lambda layer: [_LayerLanes(packed, layer, blk, w) for blk, w in blocks]


def kernel(x, c, ctx, c_ctx, mod_w, mod_b, pre1_g, post1_g, pre2_g, post2_g, w_in, conv_w, conv_b,
           conv_ln_g, conv_ln_b, ret_log_decay, ret_gn_g, mla_q_norm_g, mla_w_uq, mla_kv_norm_g,
           mla_w_ukv, w_out, ffn_w_in, ffn_w_out):
    B, T, D = x.shape
    TC = ctx.shape[1]
    cv = jnp.concatenate([c, c_ctx[None], jnp.zeros((MOD_ROWS - B - 1, D), F32)], axis=0)
    mod_table = _modulation(cv, mod_w, mod_b).reshape(DEPTH * MOD_ROWS * 6, 1, D)

    tabs_lat = _rope_tables(T, True)
    tabs_ctx = _rope_tables(TC, False)
    zstate = jnp.zeros((B, RET_HEADS // 2, LANES, LANES), F32)

    stacks = _stacked_weights(w_in, mla_w_uq, mla_w_ukv, w_out, ffn_w_in, ffn_w_out)
    stacks += (jnp.pad(conv_w, ((0, 0), (0, 1), (0, 0))),)
    lg_lanes = jnp.repeat(ret_log_decay, RET_DK, axis=-1)
    vectors = _pack_vectors(pre1_g, post1_g, pre2_g, post2_g, mla_q_norm_g, ret_gn_g, lg_lanes[:, 0],
                            lg_lanes[:, 1], conv_b, conv_ln_g, conv_ln_b, mla_kv_norm_g)

    xc = ctx
    for l in range(DEPTH):
        last = l == DEPTH - 1
        w_ext, wuq, wukv, wo, w1, w2, cw = [_Layered(a, l) for a in stacks]
        pre1, post1, pre2, post2, qng, gng, lgf, lgb, cb, clg, clb, kvg = vectors(l)
        lat = _ModRows(mod_table, l * MOD_ROWS * 6, 6)
        cx = _ModRows(mod_table, (l * MOD_ROWS + B) * 6, 0)
        lgd = ret_log_decay[l]

        proj = lambda xx, md, tabs: _inproj(xx, md, pre1, w_ext, qng, wuq, kvg, wukv, tabs)
        aL, rqL, rkL, rvL, rgL, mqL, mkL, mveL, mvoL = proj(x, lat, tabs_lat)
        aC, rqC, rkC, rvC, rgC, mqC, mkC, mveC, mvoC = proj(xc, cx, tabs_ctx)

        yrC, sf, sb = _retention(lgd, rqC, rkC, rvC, rgC, gng, lgf, lgb, zstate, zstate)
        yrL, _, _ = _retention(lgd, rqL, rkL, rvL, rgL, gng, lgf, lgb, sf, sb)

        attL = _attention(mqL, [(mkC, mveC, mvoC), (mkL, mveL, mvoL)])
        mix = lambda a, yr, att, md, xx: _mixer(a, cw, cb, clg, clb, yr, att, wo, post1, md, xx)
        ffn = lambda xx, md: _ffn(xx, md, pre2, w1, w2, post2)
        x = mix(aL, yrL, attL, lat, x)
        if not last:
            attC = _attention(mqC, [(mkC, mveC, mvoC)])
            xc = mix(aC, yrC, attC, cx, xc)
        x = ffn(x, lat)
        if not last:
            xc = ffn(xc, cx)
    return x
```

```python
import functools

import numpy as np
import jax
import jax.numpy as jnp
from jax import lax
from jax.experimental import pallas as pl
from jax.experimental.pallas import tpu as pltpu

F32 = jnp.float32
BF16 = jnp.bfloat16

DEPTH = 2
GRID_W = 64
EPS = 1e-6
LOG2E = 1.4426950408889634
ROPE_BASE = 10000.0
CONV_CH = 256
CONV_K = 31
RET_HEADS = 4
RET_DK = 64
RET_DV = 64
MLA_HEADS = 8
MLA_Q_RANK = 256
MLA_KV_RANK = 128
MLA_NOPE = 64
MLA_ROPE = 32
MLA_V = 64
D_FF = 2816

LANES = 128
SUBLANES = 8
HALO = 16
CONV_SPAN = 24
ROW_TILE = 512
MIX_TILE = 1024
FFN_TILE = 1024
FFN_SUB = 512
RET_CHUNK = 256
ATT_TQ = 512
ATT_TK = 512
ATT_AHEAD = 2
ATT_SLOTS = ATT_AHEAD + 1
ATT_ROWS = 32
MXU_DIM = 256
FF_CHUNKS = ((0, 6 * MXU_DIM), (6 * MXU_DIM, D_FF))
VMEM_LIMIT = 56 * 1024 * 1024

RET_W = RET_HEADS * RET_DK
_OFF_A = 0
_OFF_Q = _OFF_A + 2 * CONV_CH
_OFF_K = _OFF_Q + RET_W
_OFF_V = _OFF_K + RET_W
_OFF_G = _OFF_V + RET_W
_OFF_CQ = _OFF_G + RET_W
_OFF_CKV = _OFF_CQ + MLA_Q_RANK
_OFF_KRP = _OFF_CKV + MLA_KV_RANK
D_EXT = _OFF_KRP + LANES
MOD_ROWS = 8
MQ_W = MLA_HEADS * LANES
ATT_W = MLA_HEADS * MLA_V


def _sigmoid(x):
    return 1.0 / (1.0 + jnp.exp(-x))


def _silu(x):
    return x * _sigmoid(x)


def _params(*sem):
    return pltpu.CompilerParams(dimension_semantics=sem, vmem_limit_bytes=VMEM_LIMIT)


class _Layered:
    def __init__(self, arr, layer):
        self.arr, self.layer = arr, layer

    @property
    def spec(self):
        n, layer = self.arr.ndim - 1, self.layer
        return pl.BlockSpec((None,) + self.arr.shape[1:], lambda *_: (layer,) + (0,) * n,
                            pipeline_mode=pl.Buffered(1))


class _LayerLanes:
    def __init__(self, arr, layer, block, width):
        self.arr, self.layer, self.block, self.width = arr, layer, block, width

    @property
    def spec(self):
        layer, block = self.layer, self.block
        return pl.BlockSpec((None, 1, self.width), lambda *_: (layer, 0, block), pipeline_mode=pl.Buffered(1))


class _ModRows:
    def __init__(self, table, base, per_batch):
        self.table, self.base, self.per_batch = table, base, per_batch

    def spec(self, j):
        base, step = self.base + j, self.per_batch
        return pl.BlockSpec((1, 1, self.table.shape[-1]), lambda b, i: (base + step * b, 0, 0))


def _mod_kernel(cv_ref, w_ref, b_ref, o_ref):
    s = _silu(cv_ref[...])
    o_ref[0] = jnp.dot(s.astype(BF16), w_ref[0].astype(BF16), preferred_element_type=F32) + b_ref[0]


def _modulation(cv, mod_w, mod_b):
    L, D, N = mod_w.shape
    tn = 1536
    return pl.pallas_call(
        _mod_kernel,
        out_shape=jax.ShapeDtypeStruct((L, 8, N), F32),
        grid=(L, N // tn),
        in_specs=[pl.BlockSpec((8, D), lambda l, j: (0, 0)),
                  pl.BlockSpec((1, D, tn), lambda l, j: (l, 0, j)),
                  pl.BlockSpec((1, 1, tn), lambda l, j: (l, 0, j))],
        out_specs=pl.BlockSpec((1, 8, tn), lambda l, j: (l, 0, j)),
        compiler_params=_params("arbitrary", "arbitrary"),
        name="modulation",
    )(cv, mod_w, mod_b.reshape(L, 1, N))


def _rope(x, cos, sin_up, sin_dn, f):
    return x * cos + pltpu.roll(x, LANES - f, 1) * sin_up + pltpu.roll(x, f, 1) * sin_dn


def _inproj_kernel(x_ref, sh_ref, sc_ref, g_ref, w_ref, qng_ref, wuq_ref, kvg_ref, wukv_ref, tab_ref,
                   a_ref, rq_ref, rk_ref, rv_ref, rg_ref, mq_ref, mk_ref, mve_ref, mvo_ref):
    x = x_ref[0]
    h = x * lax.rsqrt(jnp.mean(x * x, axis=-1, keepdims=True) + EPS) * g_ref[...]
    hb = (h * (1.0 + sc_ref[0]) + sh_ref[0]).astype(BF16)

    def proj(lo, hi):
        return jnp.dot(hb, w_ref[:, lo:hi], preferred_element_type=F32)

    pm = proj(_OFF_CQ, D_EXT)
    pr = proj(_OFF_Q, _OFF_V)
    cq = pm[:, 0:MLA_Q_RANK]
    qn = cq * lax.rsqrt(jnp.mean(cq * cq, axis=-1, keepdims=True) + EPS) * qng_ref[...]
    qq = jnp.dot(qn.astype(BF16), wuq_ref[...], preferred_element_type=F32)
    ckv = pm[:, _OFF_CKV - _OFF_CQ:_OFF_KRP - _OFF_CQ]
    kvn = ckv * lax.rsqrt(jnp.mean(ckv * ckv, axis=-1, keepdims=True) + EPS) * kvg_ref[...]
    kv = jnp.dot(kvn.astype(BF16), wukv_ref[...], preferred_element_type=F32)
    pa = proj(_OFF_A, _OFF_Q)
    pv = proj(_OFF_V, _OFF_CQ)

    a_ref[0] = pa[:, 0:CONV_CH] * _sigmoid(pa[:, CONV_CH:2 * CONV_CH])

    ret_tabs = (tab_ref[0], tab_ref[1], tab_ref[2], RET_DK // 4)
    for i in range(2):
        lo, hi = i * LANES, (i + 1) * LANES
        rq_ref[0, :, lo:hi] = _rope(pr[:, lo:hi], *ret_tabs).astype(BF16)
        rk_ref[0, :, lo:hi] = _rope(pr[:, RET_W + lo:RET_W + hi], *ret_tabs).astype(BF16)
    rv_ref[0] = pv[:, 0:RET_W].astype(BF16)
    rg_ref[0] = pv[:, RET_W:2 * RET_W]

    mla_tabs = (tab_ref[3], tab_ref[4], tab_ref[5], MLA_ROPE // 4)
    scale = float((MLA_NOPE + MLA_ROPE) ** -0.5 * LOG2E)
    kr_slot = _rope(pm[:, _OFF_KRP - _OFF_CQ:D_EXT - _OFF_CQ], *mla_tabs)
    for hh in range(MLA_HEADS):
        lo, hi = hh * LANES, (hh + 1) * LANES
        mq_ref[0, :, lo:hi] = (_rope(qq[:, lo:hi], *mla_tabs) * scale).astype(BF16)
        mk_ref[0, :, lo:hi] = (kv[:, lo:hi] + kr_slot).astype(BF16)
    v = kv[:, MQ_W:]
    lane = lax.broadcasted_iota(jnp.int32, v.shape, 1)
    lane = lane % LANES
    even = lane < MLA_V
    mve_ref[0] = jnp.where(even, v, jnp.where(lane == MLA_V, 1.0, 0.0)).astype(BF16)
    mvo_ref[0] = jnp.where(even, jnp.where(lane == 0, 1.0, 0.0), v).astype(BF16)


def _inproj(x, mod, g, w_ext, qng, wuq, kvg, wukv, tabs):
    B, L, D = x.shape
    tm = min(ROW_TILE, L)
    row = lambda w: pl.BlockSpec((1, tm, w), lambda b, i: (b, i, 0))
    outs = [(CONV_CH, F32), (RET_W, BF16), (RET_W, BF16), (RET_W, BF16), (RET_W, F32),
            (MQ_W, BF16), (MQ_W, BF16), (ATT_W, BF16), (ATT_W, BF16)]
    layered = [g, w_ext, qng, wuq, kvg, wukv]
    return pl.pallas_call(
        _inproj_kernel,
        out_shape=[jax.ShapeDtypeStruct((B, L, w), dt) for w, dt in outs],
        grid=(B, L // tm),
        in_specs=[row(D), mod.spec(0), mod.spec(1)] + [a.spec for a in layered]
        + [pl.BlockSpec((tabs.shape[0], tm, LANES), lambda b, i: (0, i, 0))],
        out_specs=[row(w) for w, _ in outs],
        compiler_params=_params("arbitrary", "arbitrary"),
        name="inproj",
    )(x, mod.table, mod.table, *[a.arr for a in layered], tabs)


def _ret_kernel(lgd_ref, q_ref, k_ref, v_ref, g_ref, gng_ref, lgf_ref, lgb_ref, sf0_ref, sb0_ref,
                y_ref, sfo_ref, sbo_ref, w_scr, dec_scr, kvf_scr, kvb_scr, sfs_scr, sbs_scr, *, n_chunks):
    C = RET_CHUNK
    N = n_chunks
    P = RET_HEADS // 2
    U = 8 if N % 8 == 0 else 1

    @pl.when(pl.program_id(0) == 0)
    def _():
        t = lax.broadcasted_iota(jnp.int32, (C, C), 0)
        m = lax.broadcasted_iota(jnp.int32, (C, C), 1)
        d = (t - m).astype(F32)
        for hh in range(RET_HEADS):
            w_scr[hh] = jnp.exp(jnp.where(d >= 0.0, lgd_ref[0, hh] * d, lgd_ref[1, hh] * (-d)))
        j = lax.broadcasted_iota(jnp.int32, (C, LANES), 0).astype(F32)
        for pi in range(P):
            lgf = lgf_ref[:, pi * LANES:(pi + 1) * LANES]
            lgb = lgb_ref[:, pi * LANES:(pi + 1) * LANES]
            dec_scr[pi, 0] = jnp.exp(lgf * (C - 1.0 - j))
            dec_scr[pi, 1] = jnp.exp(lgb * j)
            dec_scr[pi, 2] = jnp.exp(lgf * (j + 1.0))
            dec_scr[pi, 3] = jnp.exp(lgb * (C - j))

    first = lax.broadcasted_iota(jnp.int32, (C, LANES), 1) < RET_DV
    r128 = lax.broadcasted_iota(jnp.int32, (LANES, LANES), 0)
    c128 = lax.broadcasted_iota(jnp.int32, (LANES, LANES), 1)
    blockdiag = (r128 < RET_DK) == (c128 < RET_DV)
    dn = (((1,), (1,)), ((), ()))

    def sums_one(i, pi):
        rows = pl.ds(pl.multiple_of(i * C, C), C)
        lo, hi = pi * LANES, (pi + 1) * LANES
        kf = k_ref[0, rows, lo:hi].astype(F32)
        vv = v_ref[0, rows, lo:hi]
        kzf = (kf * dec_scr[pi, 0]).T.astype(BF16)
        kzb = (kf * dec_scr[pi, 1]).T.astype(BF16)
        kvf_scr[pi, i] = jnp.where(blockdiag, jnp.dot(kzf, vv, preferred_element_type=F32), 0.0)
        kvb_scr[pi, i] = jnp.where(blockdiag, jnp.dot(kzb, vv, preferred_element_type=F32), 0.0)

    def out_one(i, pi):
        rows = pl.ds(pl.multiple_of(i * C, C), C)
        lo, hi = pi * LANES, (pi + 1) * LANES
        q = q_ref[0, rows, lo:hi]
        k = k_ref[0, rows, lo:hi]
        vv = v_ref[0, rows, lo:hi]
        zero = jnp.zeros_like(q)
        s0 = lax.dot_general(jnp.where(first, q, zero), k, dn, preferred_element_type=F32)
        s1 = lax.dot_general(jnp.where(first, zero, q), k, dn, preferred_element_type=F32)
        p0 = (s0 * w_scr[2 * pi]).astype(BF16)
        p1 = (s1 * w_scr[2 * pi + 1]).astype(BF16)
        o = jnp.dot(p0, jnp.where(first, vv, zero), preferred_element_type=F32)
        o = o + jnp.dot(p1, jnp.where(first, zero, vv), preferred_element_type=F32)
        o = o + jnp.dot(q, sfs_scr[pi, i].astype(BF16), preferred_element_type=F32) * dec_scr[pi, 2]
        o = o + jnp.dot(q, sbs_scr[pi, i].astype(BF16), preferred_element_type=F32) * dec_scr[pi, 3]
        inv = 1.0 / RET_DV
        tot = jnp.sum(o, axis=-1, keepdims=True)
        m0 = jnp.sum(jnp.where(first, o, 0.0), axis=-1, keepdims=True)
        mu = jnp.where(first, m0, tot - m0) * inv
        dlt = o - mu
        d2 = dlt * dlt
        tot2 = jnp.sum(d2, axis=-1, keepdims=True)
        v0 = jnp.sum(jnp.where(first, d2, 0.0), axis=-1, keepdims=True)
        var = jnp.where(first, v0, tot2 - v0) * inv
        y = dlt * lax.rsqrt(var + EPS) * gng_ref[:, lo:hi]
        y_ref[0, rows, lo:hi] = (y * _silu(g_ref[0, rows, lo:hi])).astype(BF16)

    def over_chunks(fn):
        def body(t, carry):
            for u in range(U):
                for pi in range(P):
                    fn(t * U + u, pi)
            return carry
        lax.fori_loop(0, N // U, body, 0)

    over_chunks(sums_one)

    gc_f = [jnp.exp(lgf_ref[:, pi * LANES:(pi + 1) * LANES] * float(C)) for pi in range(P)]
    gc_b = [jnp.exp(lgb_ref[:, pi * LANES:(pi + 1) * LANES] * float(C)) for pi in range(P)]

    def fwd_scan(i, ss):
        for pi in range(P):
            sfs_scr[pi, i] = ss[pi]
        return tuple(ss[pi] * gc_f[pi] + kvf_scr[pi, i] for pi in range(P))

    def bwd_scan(ii, ss):
        i = N - 1 - ii
        for pi in range(P):
            sbs_scr[pi, i] = ss[pi]
        return tuple(ss[pi] * gc_b[pi] + kvb_scr[pi, i] for pi in range(P))

    sf = lax.fori_loop(0, N, fwd_scan, tuple(sf0_ref[0, pi] for pi in range(P)))
    sb = lax.fori_loop(0, N, bwd_scan, tuple(sb0_ref[0, pi] for pi in range(P)))
    for pi in range(P):
        sfo_ref[0, pi] = sf[pi]
        sbo_ref[0, pi] = sb[pi]

    over_chunks(out_one)


def _retention(lgd, q, k, v, g, gng, lgf, lgb, sf0, sb0):
    B, L, W = q.shape
    C = RET_CHUNK
    N = L // C
    P = RET_HEADS // 2
    seq = lambda: pl.BlockSpec((1, L, W), lambda b: (b, 0, 0))
    st = lambda: pl.BlockSpec((1, P, LANES, LANES), lambda b: (b, 0, 0, 0))
    st_shape = jax.ShapeDtypeStruct((B, P, LANES, LANES), F32)
    return pl.pallas_call(
        functools.partial(_ret_kernel, n_chunks=N),
        out_shape=[jax.ShapeDtypeStruct((B, L, W), BF16), st_shape, st_shape],
        grid=(B,),
        in_specs=[pl.BlockSpec(memory_space=pltpu.SMEM), seq(), seq(), seq(), seq(),
                  gng.spec, lgf.spec, lgb.spec, st(), st()],
        out_specs=[seq(), st(), st()],
        scratch_shapes=[pltpu.VMEM((RET_HEADS, C, C), F32), pltpu.VMEM((P, 4, C, LANES), F32)]
        + [pltpu.VMEM((P, N, LANES, LANES), F32) for _ in range(4)],
        compiler_params=_params("arbitrary"),
        name="retention",
    )(lgd, q, k, v, g, gng.arr, lgf.arr, lgb.arr, sf0, sb0)


def _attn_kernel(*refs, lengths):
    q_ref = refs[0]
    src = refs[1:1 + 3 * len(lengths)]
    o_ref, s_scr, p_scr, m_scr, a_scr, acc_scr = refs[1 + 3 * len(lengths):]
    tq = q_ref.shape[1]
    dn = (((1,), (1,)), ((), ()))
    H = MLA_HEADS
    RB = ATT_ROWS

    m_scr[...] = jnp.full(m_scr.shape, -1e30, F32)
    acc_scr[...] = jnp.zeros(acc_scr.shape, F32)

    def step(k_ref, ve_ref, vo_ref, rows, tk):
        def scores(h):
            q = q_ref[0, :, h * LANES:(h + 1) * LANES]
            k = k_ref[0, rows, h * LANES:(h + 1) * LANES]
            s_scr[h % ATT_SLOTS, :, 0:tk] = lax.dot_general(q, k, dn, preferred_element_type=F32)

        for h in range(ATT_AHEAD):
            scores(h)
        for h in range(H):
            if h + ATT_AHEAD < H:
                scores(h + ATT_AHEAD)
            for r0 in range(0, tq, RB):
                s = s_scr[h % ATT_SLOTS, r0:r0 + RB, 0:tk]
                m_old = m_scr[h, r0:r0 + RB, :]
                n = jnp.maximum(m_old, jnp.max(s, axis=-1, keepdims=True))
                p_scr[h % 2, r0:r0 + RB, 0:tk] = jnp.exp2(s - jnp.tile(n, (1, tk // LANES))).astype(BF16)
                a_scr[h % 2, r0:r0 + RB, :] = jnp.exp2(m_old - n)
                m_scr[h, r0:r0 + RB, :] = n
            v_ref = ve_ref if h % 2 == 0 else vo_ref
            v = v_ref[0, rows, (h // 2) * LANES:(h // 2 + 1) * LANES]
            acc_scr[h] = acc_scr[h] * a_scr[h % 2] + jnp.dot(p_scr[h % 2, :, 0:tk], v, preferred_element_type=F32)

    for si, length in enumerate(lengths):
        k_ref, ve_ref, vo_ref = src[3 * si:3 * si + 3]
        tk = min(ATT_TK, length)
        n = length // tk
        if n == 1:
            step(k_ref, ve_ref, vo_ref, pl.ds(0, tk), tk)
        else:
            @pl.loop(0, n)
            def _(c):
                step(k_ref, ve_ref, vo_ref, pl.ds(pl.multiple_of(c * tk, tk), tk), tk)

    first = lax.broadcasted_iota(jnp.int32, (tq, LANES), 1) < MLA_V
    for pj in range(H // 2):
        a0, a1 = acc_scr[2 * pj], acc_scr[2 * pj + 1]
        out = jnp.where(first, a0 / a0[:, MLA_V:MLA_V + 1], a1 / a1[:, 0:1])
        o_ref[0, :, pj * LANES:(pj + 1) * LANES] = out.astype(BF16)


def _attention(q, sources):
    B, L, _ = q.shape
    tq = min(ATT_TQ, L)
    lengths =tuple(s[0].shape[1] for s in sources)
    in_specs = [pl.BlockSpec((1, tq, MQ_W), lambda b, i: (b, i, 0))]
    args = [q]
    for k, ve, vo in sources:
        S = k.shape[1]
        in_specs += [pl.BlockSpec((1, S, MQ_W), lambda b, i: (b, 0, 0)),
                     pl.BlockSpec((1, S, ATT_W), lambda b, i: (b, 0, 0)),
                     pl.BlockSpec((1, S, ATT_W), lambda b, i: (b, 0, 0))]
        args += [k, ve, vo]
    return pl.pallas_call(
        functools.partial(_attn_kernel, lengths=lengths),
        out_shape=jax.ShapeDtypeStruct((B, L, MLA_HEADS * MLA_V), BF16),
        grid=(B, L // tq),
        in_specs=in_specs,
        out_specs=pl.BlockSpec((1, tq, MLA_HEADS * MLA_V), lambda b, i: (b, i, 0)),
        scratch_shapes=[pltpu.VMEM((ATT_SLOTS, tq, ATT_TK), F32),
                        pltpu.VMEM((2, tq, ATT_TK), BF16),
                        pltpu.VMEM((MLA_HEADS, tq, LANES), F32),
                        pltpu.VMEM((2, tq, LANES), F32),
                        pltpu.VMEM((MLA_HEADS, tq, LANES), F32)],
        compiler_params=_params("arbitrary", "arbitrary"),
        name="attention",
    )(*args)


def _mixer_kernel(a_ref, ap_ref, an_ref, cw_ref, cb_ref, lg_ref, lb_ref, yr_ref, at_ref, wo_ref,
                  pg_ref, gate_ref, x_ref, o_ref, win_scr, shf_scr, yc_scr):
    tm = a_ref.shape[1]
    i = pl.program_id(1)
    n = pl.num_programs(1)
    win_scr[0:HALO] = jnp.where(i > 0, ap_ref[0], 0.0)
    win_scr[HALO:HALO + tm] = a_ref[0]
    win_scr[HALO + tm:HALO + tm + HALO] = jnp.where(i < n - 1, an_ref[0], 0.0)
    span = tm + CONV_SPAN
    for ph in range(1, SUBLANES):
        shf_scr[ph - 1] = win_scr[ph:ph + span, :]
    sub = 128
    off = HALO - CONV_K // 2
    for r0 in range(0, tm, sub):
        acc = jnp.zeros((sub, CONV_CH), F32) + cb_ref[...]
        for t in range(CONV_K):
            ph = (off + t) % SUBLANES
            base = r0 + off + t - ph
            rows = win_scr[base:base + sub, :] if ph == 0 else shf_scr[ph - 1, base:base + sub, :]
            acc = acc + cw_ref[t:t + 1, :] * rows
        mu = jnp.mean(acc, axis=-1, keepdims=True)
        d = acc - mu
        var = jnp.mean(d * d, axis=-1, keepdims=True)
        yc = d * lax.rsqrt(var + EPS) * lg_ref[...] + lb_ref[...]
        yc_scr[r0:r0 + sub, :] = _silu(yc).astype(BF16)
    r1, r2 = CONV_CH, CONV_CH + RET_W
    y = jnp.dot(yc_scr[...], wo_ref[0:r1, :], preferred_element_type=F32)
    y = y + jnp.dot(yr_ref[0], wo_ref[r1:r2, :], preferred_element_type=F32)
    y = y + jnp.dot(at_ref[0], wo_ref[r2:r2 + ATT_W, :], preferred_element_type=F32)
    r = y * lax.rsqrt(jnp.mean(y * y, axis=-1, keepdims=True) + EPS) * pg_ref[...]
    o_ref[0] = x_ref[0] + gate_ref[0] * r


def _mixer(a, cw, cb, lg, lb, yr, att, wo, pg, mod, x):
    B, L, D = x.shape
    tm = min(MIX_TILE, L)
    hb = tm // HALO
    nh = L // HALO
    row = lambda w: pl.BlockSpec((1, tm, w), lambda b, i: (b, i, 0))
    return pl.pallas_call(
        _mixer_kernel,
        out_shape=jax.ShapeDtypeStruct((B, L, D), F32),
        grid=(B, L // tm),
        in_specs=[row(CONV_CH),
                  pl.BlockSpec((1, HALO, CONV_CH), lambda b, i: (b, jnp.maximum(i * hb - 1, 0), 0)),
                  pl.BlockSpec((1, HALO, CONV_CH), lambda b, i: (b, jnp.minimum((i + 1) * hb, nh - 1), 0)),
                  cw.spec, cb.spec, lg.spec, lb.spec, row(RET_W), row(ATT_W), wo.spec, pg.spec, mod.spec(2), row(D)],
        out_specs=row(D),
        scratch_shapes=[pltpu.VMEM((tm + 2 * HALO, CONV_CH), F32),
                        pltpu.VMEM((SUBLANES - 1, tm + CONV_SPAN, CONV_CH), F32),
                        pltpu.VMEM((tm, CONV_CH), BF16)],
        compiler_params=_params("arbitrary", "arbitrary"),
        name="mixer_out",
    )(a, a, a, cw.arr, cb.arr, lg.arr, lb.arr, yr, att, wo.arr, pg.arr, mod.table, x)


def _ffn_kernel(x_ref, sh_ref, sc_ref, g_ref, w1_ref, w2_ref, pg_ref, gate_ref, o_ref):
    sub = min(FFN_SUB, x_ref.shape[1])
    for r0 in range(0, x_ref.shape[1], sub):
        x = x_ref[0, r0:r0 + sub, :]
        h = x * lax.rsqrt(jnp.mean(x * x, axis=-1, keepdims=True) + EPS) * g_ref[...]
        h = (h * (1.0 + sc_ref[0]) + sh_ref[0]).astype(BF16)
        y = jnp.zeros(x.shape, F32)
        for lo, hi in FF_CHUNKS:
            u = jnp.dot(h, w1_ref[:, lo:hi], preferred_element_type=F32)
            gt = jnp.dot(h, w1_ref[:, D_FF + lo:D_FF + hi], preferred_element_type=F32)
            y = y + jnp.dot((_silu(gt) * u).astype(BF16), w2_ref[lo:hi, :], preferred_element_type=F32)
        r = y * lax.rsqrt(jnp.mean(y * y, axis=-1, keepdims=True) + EPS) * pg_ref[...]
        o_ref[0, r0:r0 + sub, :] = x + gate_ref[0] * r


def _ffn(x, mod, g, w1, w2, pg):
    B, L, D = x.shape
    tm = min(FFN_TILE, L)
    row = pl.BlockSpec((1, tm, D), lambda b, i: (b, i, 0))
    return pl.pallas_call(
        _ffn_kernel,
        out_shape=jax.ShapeDtypeStruct((B, L, D), F32),
        grid=(B, L // tm),
        in_specs=[row, mod.spec(3), mod.spec(4), g.spec, w1.spec, w2.spec, pg.spec, mod.spec(5)],
        out_specs=row,
        compiler_params=_params("arbitrary", "arbitrary"),
        name="ffn",
    )(x, mod.table, mod.table, g.arr, w1.arr, w2.arr, pg.arr, mod.table)


def _rope_tables(T, rotary):
    t = np.arange(T)
    rowp = (t // GRID_W).astype(np.float64)[:, None]
    colp = (t % GRID_W).astype(np.float64)[:, None]

    def half(f):
        inv = ROPE_BASE ** (-np.arange(f, dtype=np.float64) / f)
        ar, ac = rowp * inv[None, :], colp * inv[None, :]
        cos = np.concatenate([np.cos(ar)] * 2 + [np.cos(ac)] * 2, axis=-1)
        sin_r, sin_c, z = np.sin(ar), np.sin(ac), np.zeros_like(ar)
        sin_up = np.concatenate([-sin_r, z, -sin_c, z], axis=-1)
        sin_dn = np.concatenate([z, sin_r, z, sin_c], axis=-1)
        if not rotary:
            cos, sin_up, sin_dn = np.ones_like(cos), np.zeros_like(cos), np.zeros_like(cos)
        return cos, sin_up, sin_dn

    ret = [np.tile(a, (1, 2)) for a in half(RET_DK // 4)]

    def pad(a, fill):
        return np.concatenate([np.full((T, MLA_NOPE), fill), a,
                               np.full((T, LANES - MLA_NOPE - MLA_ROPE), fill)], axis=-1)

    cm, su, sd = half(MLA_ROPE // 4)
    return jnp.asarray(np.stack(ret + [pad(cm, 1.0), pad(su, 0.0), pad(sd, 0.0)]).astype(np.float32))


def _stacked_weights(w_in, mla_w_uq, mla_w_ukv, w_out, ffn_w_in, ffn_w_out):
    L, D, _ = w_in.shape
    offs = [0]
    for s in (2 * CONV_CH, RET_W, RET_W, RET_W, RET_W, MLA_Q_RANK, MLA_KV_RANK, MLA_ROPE):
        offs.append(offs[-1] + s)
    wa, wq, wk, wv, wg, wcq, wckv, wkr = [w_in[:, :, offs[i]:offs[i + 1]] for i in range(8)]
    zl = jnp.zeros((L, D, MLA_NOPE), F32)
    zr = jnp.zeros((L, D, LANES - MLA_NOPE - MLA_ROPE), F32)
    w_ext = jnp.concatenate([wa, wq, wk * (RET_DK ** -0.5), wv, wg, wcq, wckv, zl, wkr, zr], axis=2).astype(BF16)

    uq = mla_w_uq.reshape(L, MLA_Q_RANK, MLA_HEADS, MLA_NOPE + MLA_ROPE)
    zq = jnp.zeros((L, MLA_Q_RANK, MLA_HEADS, LANES - MLA_NOPE - MLA_ROPE), F32)
    wuq = jnp.concatenate([uq, zq], axis=-1).reshape(L, MLA_Q_RANK, MQ_W).astype(BF16)

    ukv = mla_w_ukv.reshape(L, MLA_KV_RANK, MLA_HEADS, MLA_NOPE + MLA_V)
    zk = jnp.zeros((L, MLA_KV_RANK, MLA_HEADS, LANES - MLA_NOPE), F32)
    uk_pad = jnp.concatenate([ukv[..., :MLA_NOPE], zk], axis=-1).reshape(L, MLA_KV_RANK, MQ_W)
    uv = ukv[..., MLA_NOPE:].reshape(L, MLA_KV_RANK, MLA_HEADS * MLA_V)
    wukv = jnp.concatenate([uk_pad, uv], axis=2).astype(BF16)
    return w_ext, wuq, wukv, w_out.astype(BF16), ffn_w_in.astype(BF16), ffn_w_out.astype(BF16)


def _pack_vectors(*vectors):
    packed = jnp.concatenate(vectors, axis=-1)[:, None, :]
    blocks, off = [], 0
    for v in vectors:
        w = v.shape[-1]
        assert off % w == 0
        blocks.append((off // w, w))
        off += w
    return lambda layer: [_LayerLanes(packed, layer, blk, w) for blk, w in blocks]


def kernel(x, c, ctx, c_ctx, mod_w, mod_b, pre1_g, post1_g, pre2_g, post2_g, w_in, conv_w, conv_b,
           conv_ln_g, conv_ln_b, ret_log_decay, ret_gn_g, mla_q_norm_g, mla_w_uq, mla_kv_norm_g,
           mla_w_ukv, w_out, ffn_w_in, ffn_w_out):
    B, T, D = x.shape
    TC = ctx.shape[1]
    cv = jnp.concatenate([c, c_ctx[None], jnp.zeros((MOD_ROWS - B - 1, D), F32)], axis=0)
    mod_table = _modulation(cv, mod_w, mod_b).reshape(DEPTH * MOD_ROWS * 6, 1, D)

    tabs_lat = _rope_tables(T, True)
    tabs_ctx = _rope_tables(TC, False)
    zstate = jnp.zeros((B, RET_HEADS // 2, LANES, LANES), F32)

    stacks = _stacked_weights(w_in, mla_w_uq, mla_w_ukv, w_out, ffn_w_in, ffn_w_out)
    stacks += (jnp.pad(conv_w, ((0, 0), (0, 1), (0, 0))),)
    lg_lanes = jnp.repeat(ret_log_decay, RET_DK, axis=-1)
    vectors = _pack_vectors(pre1_g, post1_g, pre2_g, post2_g, mla_q_norm_g, ret_gn_g, lg_lanes[:, 0],
                            lg_lanes[:, 1], conv_b, conv_ln_g, conv_ln_b, mla_kv_norm_g)

    xc = ctx
    for l in range(DEPTH):
        last = l == DEPTH - 1
        w_ext, wuq, wukv, wo, w1, w2, cw = [_Layered(a, l) for a in stacks]
        pre1, post1, pre2, post2, qng, gng, lgf, lgb, cb, clg, clb, kvg = vectors(l)
        lat = _ModRows(mod_table, l * MOD_ROWS * 6, 6)
        cx = _ModRows(mod_table, (l * MOD_ROWS + B) * 6, 0)
        lgd = ret_log_decay[l]

        proj = lambda xx, md, tabs: _inproj(xx, md, pre1, w_ext, qng, wuq, kvg, wukv, tabs)
        aL, rqL, rkL, rvL, rgL, mqL, mkL, mveL, mvoL = proj(x, lat, tabs_lat)
        aC, rqC, rkC, rvC, rgC, mqC, mkC, mveC, mvoC = proj(xc, cx, tabs_ctx)

        yrC, sf, sb = _retention(lgd, rqC, rkC, rvC, rgC, gng, lgf, lgb, zstate, zstate)
        yrL, _, _ = _retention(lgd, rqL, rkL, rvL, rgL, gng, lgf, lgb, sf, sb)

        attL = _attention(mqL, [(mkC, mveC, mvoC), (mkL, mveL, mvoL)])
        mix = lambda a, yr, att, md, xx: _mixer(a, cw, cb, clg, clb, yr, att, wo, post1, md, xx)
        ffn = lambda xx, md: _ffn(xx, md, pre2, w1, w2, post2)
        x = mix(aL, yrL, attL, lat, x)
        if not last:
            attC = _attention(mqC, [(mkC, mveC, mvoC)])
            xc = mix(aC, yrC, attC, cx, xc)
        x = ffn(x, lat)
        if not last:
            xc = ffn(xc, cx)
    return x
```

```python
import functools

import numpy as np
import jax
import jax.numpy as jnp
from jax import lax
from jax.experimental import pallas as pl
from jax.experimental.pallas import tpu as pltpu

F32 = jnp.float32
BF16 = jnp.bfloat16

DEPTH = 2
GRID_W = 64
EPS = 1e-6
LOG2E = 1.4426950408889634
ROPE_BASE = 10000.0
CONV_CH = 256
CONV_K = 31
RET_HEADS = 4
RET_DK = 64
RET_DV = 64
MLA_HEADS = 8
MLA_Q_RANK = 256
MLA_KV_RANK = 128
MLA_NOPE = 64
MLA_ROPE = 32
MLA_V = 64
D_FF = 2816

LANES = 128
SUBLANES = 8
HALO = 16
CONV_SPAN = 24
ROW_TILE = 1024
ROW_SUB = 512
MIX_TILE = 1024
MIX_SUB = 512
FFN_TILE = 1024
FFN_SUB = 256
RET_CHUNK = 256
ATT_TQ = 512
ATT_TK = 512
ATT_AHEAD = 2
ATT_SLOTS = ATT_AHEAD + 1
ATT_ROWS = 32
MXU_DIM = 256
FF_CHUNKS = ((0, 6 * MXU_DIM), (6 * MXU_DIM, D_FF))
VMEM_LIMIT = 56 * 1024 * 1024

RET_W = RET_HEADS * RET_DK
_OFF_A = 0
_OFF_Q = _OFF_A + 2 * CONV_CH
_OFF_K = _OFF_Q + RET_W
_OFF_V = _OFF_K + RET_W
_OFF_G = _OFF_V + RET_W
_OFF_CQ = _OFF_G + RET_W
_OFF_CKV = _OFF_CQ + MLA_Q_RANK
_OFF_KRP = _OFF_CKV + MLA_KV_RANK
D_EXT = _OFF_KRP + LANES
MOD_ROWS = 8
MQ_W = MLA_HEADS * LANES
ATT_W = MLA_HEADS * MLA_V


def _sigmoid(x):
    return 1.0 / (1.0 + jnp.exp(-x))


def _silu(x):
    return x * _sigmoid(x)


def _params(*sem):
    return pltpu.CompilerParams(dimension_semantics=sem, vmem_limit_bytes=VMEM_LIMIT)


class _Layered:
    def __init__(self, arr, layer):
        self.arr, self.layer = arr, layer

    @property
    def spec(self):
        n, layer = self.arr.ndim - 1, self.layer
        return pl.BlockSpec((None,) + self.arr.shape[1:], lambda *_: (layer,) + (0,) * n,
                            pipeline_mode=pl.Buffered(1))


class _LayerLanes:
    def __init__(self, arr, layer, block, width):
        self.arr, self.layer, self.block, self.width = arr, layer, block, width

    @property
    def spec(self):
        layer, block = self.layer, self.block
        return pl.BlockSpec((None, 1, self.width), lambda *_: (layer, 0, block), pipeline_mode=pl.Buffered(1))


class _ModRows:
    def __init__(self, table, base, per_batch):
        self.table, self.base, self.per_batch = table, base, per_batch

    def spec(self, j):
        base, step = self.base + j, self.per_batch
        return pl.BlockSpec((1, 1, self.table.shape[-1]), lambda b, i: (base + step * b, 0, 0))


def _mod_kernel(cv_ref, w_ref, b_ref, o_ref):
    s = _silu(cv_ref[...])
    o_ref[0] = jnp.dot(s.astype(BF16), w_ref[0].astype(BF16), preferred_element_type=F32) + b_ref[0]


def _modulation(cv, mod_w, mod_b):
    L, D, N = mod_w.shape
    tn = 1536
    return pl.pallas_call(
        _mod_kernel,
        out_shape=jax.ShapeDtypeStruct((L, 8, N), F32),
        grid=(L, N // tn),
        in_specs=[pl.BlockSpec((8, D), lambda l, j: (0, 0)),
                  pl.BlockSpec((1, D, tn), lambda l, j: (l, 0, j)),
                  pl.BlockSpec((1, 1, tn), lambda l, j: (l, 0, j))],
        out_specs=pl.BlockSpec((1, 8, tn), lambda l, j: (l, 0, j)),
        compiler_params=_params("arbitrary", "arbitrary"),
        name="modulation",
    )(cv, mod_w, mod_b.reshape(L, 1, N))


def _rope(x, cos, sin_up, sin_dn, f):
    return x * cos + pltpu.roll(x, LANES - f, 1) * sin_up + pltpu.roll(x, f, 1) * sin_dn


def _inproj_kernel(x_ref, sh_ref, sc_ref, g_ref, w_ref, qng_ref, wuq_ref, kvg_ref, wukv_ref, tab_ref,
                   a_ref, rq_ref, rk_ref, rv_ref, rg_ref, mq_ref, mk_ref, mve_ref, mvo_ref):
    sub = min(ROW_SUB, x_ref.shape[1])
    for r0 in range(0, x_ref.shape[1], sub):
        _inproj_rows(slice(r0, r0 + sub), x_ref, sh_ref, sc_ref, g_ref, w_ref, qng_ref, wuq_ref, kvg_ref,
                     wukv_ref, tab_ref, a_ref, rq_ref, rk_ref, rv_ref, rg_ref, mq_ref, mk_ref, mve_ref, mvo_ref)


def _inproj_rows(rows, x_ref, sh_ref, sc_ref, g_ref, w_ref, qng_ref, wuq_ref, kvg_ref, wukv_ref, tab_ref,
                 a_ref, rq_ref, rk_ref, rv_ref, rg_ref, mq_ref, mk_ref, mve_ref, mvo_ref):
    x = x_ref[0, rows]
    h = x * lax.rsqrt(jnp.mean(x * x, axis=-1, keepdims=True) + EPS) * g_ref[...]
    hb = (h * (1.0 + sc_ref[0]) + sh_ref[0]).astype(BF16)

    def proj(lo, hi):
        return jnp.dot(hb, w_ref[:, lo:hi], preferred_element_type=F32)

    pm = proj(_OFF_CQ, D_EXT)
    pr = proj(_OFF_Q, _OFF_V)
    cq = pm[:, 0:MLA_Q_RANK]
    qn = cq * lax.rsqrt(jnp.mean(cq * cq, axis=-1, keepdims=True) + EPS) * qng_ref[...]
    qq = jnp.dot(qn.astype(BF16), wuq_ref[...], preferred_element_type=F32)
    ckv = pm[:, _OFF_CKV - _OFF_CQ:_OFF_KRP - _OFF_CQ]
    kvn = ckv * lax.rsqrt(jnp.mean(ckv * ckv, axis=-1, keepdims=True) + EPS) * kvg_ref[...]
    kv = jnp.dot(kvn.astype(BF16), wukv_ref[...], preferred_element_type=F32)
    pa = proj(_OFF_A, _OFF_Q)
    pv = proj(_OFF_V, _OFF_CQ)

    a_ref[0, rows] = pa[:, 0:CONV_CH] * _sigmoid(pa[:, CONV_CH:2 * CONV_CH])

    ret_tabs = (tab_ref[0, rows], tab_ref[1, rows], tab_ref[2, rows], RET_DK // 4)
    for i in range(2):
        lo, hi = i * LANES, (i + 1) * LANES
        rq_ref[0, rows, lo:hi] = _rope(pr[:, lo:hi], *ret_tabs).astype(BF16)
        rk_ref[0, rows, lo:hi] = _rope(pr[:, RET_W + lo:RET_W + hi], *ret_tabs).astype(BF16)
    rv_ref[0, rows] = pv[:, 0:RET_W].astype(BF16)
    rg_ref[0, rows] = pv[:, RET_W:2 * RET_W]

    mla_tabs = (tab_ref[3, rows], tab_ref[4, rows], tab_ref[5, rows], MLA_ROPE // 4)
    scale = float((MLA_NOPE + MLA_ROPE) ** -0.5 * LOG2E)
    kr_slot = _rope(pm[:, _OFF_KRP - _OFF_CQ:D_EXT - _OFF_CQ], *mla_tabs)
    for hh in range(MLA_HEADS):
        lo, hi = hh * LANES, (hh + 1) * LANES
        mq_ref[0, rows, lo:hi] = (_rope(qq[:, lo:hi], *mla_tabs) * scale).astype(BF16)
        mk_ref[0, rows, lo:hi] = (kv[:, lo:hi] + kr_slot).astype(BF16)
    v = kv[:, MQ_W:]
    lane = lax.broadcasted_iota(jnp.int32, v.shape, 1)
    lane = lane % LANES
    even = lane < MLA_V
    mve_ref[0, rows] = jnp.where(even, v, jnp.where(lane == MLA_V, 1.0, 0.0)).astype(BF16)
    mvo_ref[0, rows] = jnp.where(even, jnp.where(lane == 0, 1.0, 0.0), v).astype(BF16)


def _inproj(x, mod, g, w_ext, qng, wuq, kvg, wukv, tabs):
    B, L, D = x.shape
    tm = min(ROW_TILE, L)
    row = lambda w: pl.BlockSpec((1, tm, w), lambda b, i: (b, i, 0))
    outs = [(CONV_CH, F32), (RET_W, BF16), (RET_W, BF16), (RET_W, BF16), (RET_W, F32),
            (MQ_W, BF16), (MQ_W, BF16), (ATT_W, BF16), (ATT_W, BF16)]
    layered = [g, w_ext, qng, wuq, kvg, wukv]
    return pl.pallas_call(
        _inproj_kernel,
        out_shape=[jax.ShapeDtypeStruct((B, L, w), dt) for w, dt in outs],
        grid=(B, L // tm),
        in_specs=[row(D), mod.spec(0), mod.spec(1)] + [a.spec for a in layered]
        + [pl.BlockSpec((tabs.shape[0], tm, LANES), lambda b, i: (0, i, 0))],
        out_specs=[row(w) for w, _ in outs],
        compiler_params=_params("arbitrary", "arbitrary"),
        name="inproj",
    )(x, mod.table, mod.table, *[a.arr for a in layered], tabs)


def _ret_kernel(lgd_ref, q_ref, k_ref, v_ref, g_ref, gng_ref, lgf_ref, lgb_ref, sf0_ref, sb0_ref,
                y_ref, sfo_ref, sbo_ref, w_scr, dec_scr, kvf_scr, kvb_scr, sfs_scr, sbs_scr, *, n_chunks):
    C = RET_CHUNK
    N = n_chunks
    P = RET_HEADS // 2
    U = 8 if N % 8 == 0 else 1

    @pl.when(pl.program_id(0) == 0)
    def _():
        t = lax.broadcasted_iota(jnp.int32, (C, C), 0)
        m = lax.broadcasted_iota(jnp.int32, (C, C), 1)
        d = (t - m).astype(F32)
        for hh in range(RET_HEADS):
            w_scr[hh] = jnp.exp(jnp.where(d >= 0.0, lgd_ref[0, hh] * d, lgd_ref[1, hh] * (-d)))
        j = lax.broadcasted_iota(jnp.int32, (C, LANES), 0).astype(F32)
        for pi in range(P):
            lgf = lgf_ref[:, pi * LANES:(pi + 1) * LANES]
            lgb = lgb_ref[:, pi * LANES:(pi + 1) * LANES]
            dec_scr[pi, 0] = jnp.exp(lgf * (C - 1.0 - j))
            dec_scr[pi, 1] = jnp.exp(lgb * j)
            dec_scr[pi, 2] = jnp.exp(lgf * (j + 1.0))
            dec_scr[pi, 3] = jnp.exp(lgb * (C - j))

    first = lax.broadcasted_iota(jnp.int32, (C, LANES), 1) < RET_DV
    r128 = lax.broadcasted_iota(jnp.int32, (LANES, LANES), 0)
    c128 = lax.broadcasted_iota(jnp.int32, (LANES, LANES), 1)
    blockdiag = (r128 < RET_DK) == (c128 < RET_DV)
    dn = (((1,), (1,)), ((), ()))

    def sums_one(i, pi):
        rows = pl.ds(pl.multiple_of(i * C, C), C)
        lo, hi = pi * LANES, (pi + 1) * LANES
        kf = k_ref[0, rows, lo:hi].astype(F32)
        vv = v_ref[0, rows, lo:hi]
        kzf = (kf * dec_scr[pi, 0]).T.astype(BF16)
        kzb = (kf * dec_scr[pi, 1]).T.astype(BF16)
        kvf_scr[pi, i] = jnp.where(blockdiag, jnp.dot(kzf, vv, preferred_element_type=F32), 0.0)
        kvb_scr[pi, i] = jnp.where(blockdiag, jnp.dot(kzb, vv, preferred_element_type=F32), 0.0)

    def out_one(i, pi):
        rows = pl.ds(pl.multiple_of(i * C, C), C)
        lo, hi = pi * LANES, (pi + 1) * LANES
        q = q_ref[0, rows, lo:hi]
        k = k_ref[0, rows, lo:hi]
        vv = v_ref[0, rows, lo:hi]
        zero = jnp.zeros_like(q)
        s0 = lax.dot_general(jnp.where(first, q, zero), k, dn, preferred_element_type=F32)
        s1 = lax.dot_general(jnp.where(first, zero, q), k, dn, preferred_element_type=F32)
        p0 = (s0 * w_scr[2 * pi]).astype(BF16)
        p1 = (s1 * w_scr[2 * pi + 1]).astype(BF16)
        o = jnp.dot(p0, jnp.where(first, vv, zero), preferred_element_type=F32)
        o = o + jnp.dot(p1, jnp.where(first, zero, vv), preferred_element_type=F32)
        o = o + jnp.dot(q, sfs_scr[pi, i].astype(BF16), preferred_element_type=F32) * dec_scr[pi, 2]
        o = o + jnp.dot(q, sbs_scr[pi, i].astype(BF16), preferred_element_type=F32) * dec_scr[pi, 3]
        inv = 1.0 / RET_DV
        tot = jnp.sum(o, axis=-1, keepdims=True)
        m0 = jnp.sum(jnp.where(first, o, 0.0), axis=-1, keepdims=True)
        mu = jnp.where(first, m0, tot - m0) * inv
        dlt = o - mu
        d2 = dlt * dlt
        tot2 = jnp.sum(d2, axis=-1, keepdims=True)
        v0 = jnp.sum(jnp.where(first, d2, 0.0), axis=-1, keepdims=True)
        var = jnp.where(first, v0, tot2 - v0) * inv
        y = dlt * lax.rsqrt(var + EPS) * gng_ref[:, lo:hi]
        y_ref[0, rows, lo:hi] = (y * _silu(g_ref[0, rows, lo:hi])).astype(BF16)

    def over_chunks(fn):
        def body(t, carry):
            for u in range(U):
                for pi in range(P):
                    fn(t * U + u, pi)
            return carry
        lax.fori_loop(0, N // U, body, 0)

    over_chunks(sums_one)

    gc_f = [jnp.exp(lgf_ref[:, pi * LANES:(pi + 1) * LANES] * float(C)) for pi in range(P)]
    gc_b = [jnp.exp(lgb_ref[:, pi * LANES:(pi + 1) * LANES] * float(C)) for pi in range(P)]

    def fwd_scan(i, ss):
        for pi in range(P):
            sfs_scr[pi, i] = ss[pi]
        return tuple(ss[pi] * gc_f[pi] + kvf_scr[pi, i] for pi in range(P))

    def bwd_scan(ii, ss):
        i = N - 1 - ii
        for pi in range(P):
            sbs_scr[pi, i] = ss[pi]
        return tuple(ss[pi] * gc_b[pi] + kvb_scr[pi, i] for pi in range(P))

    sf = lax.fori_loop(0, N, fwd_scan, tuple(sf0_ref[0, pi] for pi in range(P)))
    sb = lax.fori_loop(0, N, bwd_scan, tuple(sb0_ref[0, pi] for pi in range(P)))
    for pi in range(P):
        sfo_ref[0, pi] = sf[pi]
        sbo_ref[0, pi] = sb[pi]

    over_chunks(out_one)


def _retention(lgd, q, k, v, g, gng, lgf, lgb, sf0, sb0):
    B, L, W = q.shape
    C = RET_CHUNK
    N = L // C
    P = RET_HEADS // 2
    seq = lambda: pl.BlockSpec((1, L, W), lambda b: (b, 0, 0))
    st = lambda: pl.BlockSpec((1, P, LANES, LANES), lambda b: (b, 0, 0, 0))
    st_shape = jax.ShapeDtypeStruct((B, P, LANES, LANES), F32)
    return pl.pallas_call(
        functools.partial(_ret_kernel, n_chunks=N),
        out_shape=[jax.ShapeDtypeStruct((B, L, W), BF16), st_shape, st_shape],
        grid=(B,),
        in_specs=[pl.BlockSpec(memory_space=pltpu.SMEM), seq(), seq(), seq(), seq(),
                  gng.spec, lgf.spec, lgb.spec, st(), st()],
        out_specs=[seq(), st(), st()],
        scratch_shapes=[pltpu.VMEM((RET_HEADS, C, C), F32), pltpu.VMEM((P, 4, C, LANES), F32)]
        + [pltpu.VMEM((P, N, LANES, LANES), F32) for _ in range(4)],
        compiler_params=_params("arbitrary"),
        name="retention",
    )(lgd, q, k, v, g, gng.arr, lgf.arr, lgb.arr, sf0, sb0)


def _attn_kernel(*refs, lengths):
    q_ref = refs[0]
    src = refs[1:1 + 3 * len(lengths)]
    o_ref, s_scr, p_scr, m_scr, a_scr, acc_scr = refs[1 + 3 * len(lengths):]
    tq = q_ref.shape[1]
    dn = (((1,), (1,)), ((), ()))
    H = MLA_HEADS
    RB = ATT_ROWS

    m_scr[...] = jnp.full(m_scr.shape, -1e30, F32)
    acc_scr[...] = jnp.zeros(acc_scr.shape, F32)

    def step(k_ref, ve_ref, vo_ref, rows, tk):
        def scores(h):
            q = q_ref[0, :, h * LANES:(h + 1) * LANES]
            k = k_ref[0, rows, h * LANES:(h + 1) * LANES]
            s_scr[h % ATT_SLOTS, :, 0:tk] = lax.dot_general(q, k, dn, preferred_element_type=F32)

        for h in range(ATT_AHEAD):
            scores(h)
        for h in range(H):
            if h + ATT_AHEAD < H:
                scores(h + ATT_AHEAD)
            for r0 in range(0, tq, RB):
                s = s_scr[h % ATT_SLOTS, r0:r0 + RB, 0:tk]
                m_old = m_scr[h, r0:r0 + RB, :]
                n = jnp.maximum(m_old, jnp.max(s, axis=-1, keepdims=True))
                p_scr[h % 2, r0:r0 + RB, 0:tk] = jnp.exp2(s - jnp.tile(n, (1, tk // LANES))).astype(BF16)
                a_scr[h % 2, r0:r0 + RB, :] = jnp.exp2(m_old - n)
                m_scr[h, r0:r0 + RB, :] = n
            v_ref = ve_ref if h % 2 == 0 else vo_ref
            v = v_ref[0, rows, (h // 2) * LANES:(h // 2 + 1) * LANES]
            acc_scr[h] = acc_scr[h] * a_scr[h % 2] + jnp.dot(p_scr[h % 2, :, 0:tk], v, preferred_element_type=F32)

    for si, length in enumerate(lengths):
        k_ref, ve_ref, vo_ref = src[3 * si:3 * si + 3]
        tk = min(ATT_TK, length)
        n = length // tk
        if n == 1:
            step(k_ref, ve_ref, vo_ref, pl.ds(0, tk), tk)
        else:
            @pl.loop(0, n)
            def _(c):
                step(k_ref, ve_ref, vo_ref, pl.ds(pl.multiple_of(c * tk, tk), tk), tk)

    first = lax.broadcasted_iota(jnp.int32, (tq, LANES), 1) < MLA_V
    for pj in range(H // 2):
        a0, a1 = acc_scr[2 * pj], acc_scr[2 * pj + 1]
        out = jnp.where(first, a0 / a0[:, MLA_V:MLA_V + 1], a1 / a1[:, 0:1])
        o_ref[0, :, pj * LANES:(pj + 1) * LANES] = out.astype(BF16)


def _attention(q, sources):
    B, L, _ = q.shape
    tq = min(ATT_TQ, L)
    lengths =tuple(s[0].shape[1] for s in sources)
    in_specs = [pl.BlockSpec((1, tq, MQ_W), lambda b, i: (b, i, 0))]
    args = [q]
    for k, ve, vo in sources:
        S = k.shape[1]
        in_specs += [pl.BlockSpec((1, S, MQ_W), lambda b, i: (b, 0, 0)),
                     pl.BlockSpec((1, S, ATT_W), lambda b, i: (b, 0, 0)),
                     pl.BlockSpec((1, S, ATT_W), lambda b, i: (b, 0, 0))]
        args += [k, ve, vo]
    return pl.pallas_call(
        functools.partial(_attn_kernel, lengths=lengths),
        out_shape=jax.ShapeDtypeStruct((B, L, MLA_HEADS * MLA_V), BF16),
        grid=(B, L // tq),
        in_specs=in_specs,
        out_specs=pl.BlockSpec((1, tq, MLA_HEADS * MLA_V), lambda b, i: (b, i, 0)),
        scratch_shapes=[pltpu.VMEM((ATT_SLOTS, tq, ATT_TK), F32),
                        pltpu.VMEM((2, tq, ATT_TK), BF16),
                        pltpu.VMEM((MLA_HEADS, tq, LANES), F32),
                        pltpu.VMEM((2, tq, LANES), F32),
                        pltpu.VMEM((MLA_HEADS, tq, LANES), F32)],
        compiler_params=_params("arbitrary", "arbitrary"),
        name="attention",
    )(*args)


def _mixer_kernel(a_ref, ap_ref, an_ref, cw_ref, cb_ref, lg_ref, lb_ref, yr_ref, at_ref, wo_ref,
                  pg_ref, gate_ref, x_ref, o_ref, win_scr, shf_scr, yc_scr):
    tm = a_ref.shape[1]
    i = pl.program_id(1)
    n = pl.num_programs(1)
    win_scr[0:HALO] = jnp.where(i > 0, ap_ref[0], 0.0)
    win_scr[HALO:HALO + tm] = a_ref[0]
    win_scr[HALO + tm:HALO + tm + HALO] = jnp.where(i < n - 1, an_ref[0], 0.0)
    span = tm + CONV_SPAN
    for ph in range(1, SUBLANES):
        shf_scr[ph - 1] = win_scr[ph:ph + span, :]
    sub = 128
    off = HALO - CONV_K // 2
    r1, r2 = CONV_CH, CONV_CH + RET_W
    for s0 in range(0, tm, MIX_SUB):
        blk = slice(s0, min(s0 + MIX_SUB, tm))
        for r0 in range(blk.start, blk.stop, sub):
            acc = jnp.zeros((sub, CONV_CH), F32) + cb_ref[...]
            for t in range(CONV_K):
                ph = (off + t) % SUBLANES
                base = r0 + off + t - ph
                rows = win_scr[base:base + sub, :] if ph == 0 else shf_scr[ph - 1, base:base + sub, :]
                acc = acc + cw_ref[t:t + 1, :] * rows
            mu = jnp.mean(acc, axis=-1, keepdims=True)
            d = acc - mu
            var = jnp.mean(d * d, axis=-1, keepdims=True)
            yc = d * lax.rsqrt(var + EPS) * lg_ref[...] + lb_ref[...]
            yc_scr[r0:r0 + sub, :] = _silu(yc).astype(BF16)
        y = jnp.dot(yc_scr[blk, :], wo_ref[0:r1, :], preferred_element_type=F32)
        y = y + jnp.dot(yr_ref[0, blk], wo_ref[r1:r2, :], preferred_element_type=F32)
        y = y + jnp.dot(at_ref[0, blk], wo_ref[r2:r2 + ATT_W, :], preferred_element_type=F32)
        r = y * lax.rsqrt(jnp.mean(y * y, axis=-1, keepdims=True) + EPS) * pg_ref[...]
        o_ref[0, blk] = x_ref[0, blk] + gate_ref[0] * r


def _mixer(a, cw, cb, lg, lb, yr, att, wo, pg, mod, x):
    B, L, D = x.shape
    tm = min(MIX_TILE, L)
    hb = tm // HALO
    nh = L // HALO
    row = lambda w: pl.BlockSpec((1, tm, w), lambda b, i: (b, i, 0))
    return pl.pallas_call(
        _mixer_kernel,
        out_shape=jax.ShapeDtypeStruct((B, L, D), F32),
        grid=(B, L // tm),
        in_specs=[row(CONV_CH),
                  pl.BlockSpec((1, HALO, CONV_CH), lambda b, i: (b, jnp.maximum(i * hb - 1, 0), 0)),
                  pl.BlockSpec((1, HALO, CONV_CH), lambda b, i: (b, jnp.minimum((i + 1) * hb, nh - 1), 0)),
                  cw.spec, cb.spec, lg.spec, lb.spec, row(RET_W), row(ATT_W), wo.spec, pg.spec, mod.spec(2), row(D)],
        out_specs=row(D),
        scratch_shapes=[pltpu.VMEM((tm + 2 * HALO, CONV_CH), F32),
                        pltpu.VMEM((SUBLANES - 1, tm + CONV_SPAN, CONV_CH), F32),
                        pltpu.VMEM((tm, CONV_CH), BF16)],
        compiler_params=_params("arbitrary", "arbitrary"),
        name="mixer_out",
    )(a, a, a, cw.arr, cb.arr, lg.arr, lb.arr, yr, att, wo.arr, pg.arr, mod.table, x)


def _ffn_kernel(x_ref, sh_ref, sc_ref, g_ref, w1_ref, w2_ref, pg_ref, gate_ref, o_ref):
    sub = min(FFN_SUB, x_ref.shape[1])
    for r0 in range(0, x_ref.shape[1], sub):
        x = x_ref[0, r0:r0 + sub, :]
        h = x * lax.rsqrt(jnp.mean(x * x, axis=-1, keepdims=True) + EPS) * g_ref[...]
        h = (h * (1.0 + sc_ref[0]) + sh_ref[0]).astype(BF16)
        y = jnp.zeros(x.shape, F32)
        for lo, hi in FF_CHUNKS:
            u = jnp.dot(h, w1_ref[:, lo:hi], preferred_element_type=F32)
            gt = jnp.dot(h, w1_ref[:, D_FF + lo:D_FF + hi], preferred_element_type=F32)
            y = y + jnp.dot((_silu(gt) * u).astype(BF16), w2_ref[lo:hi, :], preferred_element_type=F32)
        r = y * lax.rsqrt(jnp.mean(y * y, axis=-1, keepdims=True) + EPS) * pg_ref[...]
        o_ref[0, r0:r0 + sub, :] = x + gate_ref[0] * r


def _ffn(x, mod, g, w1, w2, pg):
    B, L, D = x.shape
    tm = min(FFN_TILE, L)
    row = pl.BlockSpec((1, tm, D), lambda b, i: (b, i, 0))
    return pl.pallas_call(
        _ffn_kernel,
        out_shape=jax.ShapeDtypeStruct((B, L, D), F32),
        grid=(B, L // tm),
        in_specs=[row, mod.spec(3), mod.spec(4), g.spec, w1.spec, w2.spec, pg.spec, mod.spec(5)],
        out_specs=row,
        compiler_params=_params("arbitrary", "arbitrary"),
        name="ffn",
    )(x, mod.table, mod.table, g.arr, w1.arr, w2.arr, pg.arr, mod.table)


def _rope_tables(T, rotary):
    t = np.arange(T)
    rowp = (t // GRID_W).astype(np.float64)[:, None]
    colp = (t % GRID_W).astype(np.float64)[:, None]

    def half(f):
        inv = ROPE_BASE ** (-np.arange(f, dtype=np.float64) / f)
        ar, ac = rowp * inv[None, :], colp * inv[None, :]
        cos = np.concatenate([np.cos(ar)] * 2 + [np.cos(ac)] * 2, axis=-1)
        sin_r, sin_c, z = np.sin(ar), np.sin(ac), np.zeros_like(ar)
        sin_up = np.concatenate([-sin_r, z, -sin_c, z], axis=-1)
        sin_dn = np.concatenate([z, sin_r, z, sin_c], axis=-1)
        if not rotary:
            cos, sin_up, sin_dn = np.ones_like(cos), np.zeros_like(cos), np.zeros_like(cos)
        return cos, sin_up, sin_dn

    ret = [np.tile(a, (1, 2)) for a in half(RET_DK // 4)]

    def pad(a, fill):
        return np.concatenate([np.full((T, MLA_NOPE), fill), a,
                               np.full((T, LANES - MLA_NOPE - MLA_ROPE), fill)], axis=-1)

    cm, su, sd = half(MLA_ROPE // 4)
    return jnp.asarray(np.stack(ret + [pad(cm, 1.0), pad(su, 0.0), pad(sd, 0.0)]).astype(np.float32))


def _stacked_weights(w_in, mla_w_uq, mla_w_ukv, w_out, ffn_w_in, ffn_w_out):
    L, D, _ = w_in.shape
    offs = [0]
    for s in (2 * CONV_CH, RET_W, RET_W, RET_W, RET_W, MLA_Q_RANK, MLA_KV_RANK, MLA_ROPE):
        offs.append(offs[-1] + s)
    wa, wq, wk, wv, wg, wcq, wckv, wkr = [w_in[:, :, offs[i]:offs[i + 1]] for i in range(8)]
    zl = jnp.zeros((L, D, MLA_NOPE), F32)
    zr = jnp.zeros((L, D, LANES - MLA_NOPE - MLA_ROPE), F32)
    w_ext = jnp.concatenate([wa, wq, wk * (RET_DK ** -0.5), wv, wg, wcq, wckv, zl, wkr, zr], axis=2).astype(BF16)

    uq = mla_w_uq.reshape(L, MLA_Q_RANK, MLA_HEADS, MLA_NOPE + MLA_ROPE)
    zq = jnp.zeros((L, MLA_Q_RANK, MLA_HEADS, LANES - MLA_NOPE - MLA_ROPE), F32)
    wuq = jnp.concatenate([uq, zq], axis=-1).reshape(L, MLA_Q_RANK, MQ_W).astype(BF16)

    ukv = mla_w_ukv.reshape(L, MLA_KV_RANK, MLA_HEADS, MLA_NOPE + MLA_V)
    zk = jnp.zeros((L, MLA_KV_RANK, MLA_HEADS, LANES - MLA_NOPE), F32)
    uk_pad = jnp.concatenate([ukv[..., :MLA_NOPE], zk], axis=-1).reshape(L, MLA_KV_RANK, MQ_W)
    uv = ukv[..., MLA_NOPE:].reshape(L, MLA_KV_RANK, MLA_HEADS * MLA_V)
    wukv = jnp.concatenate([uk_pad, uv], axis=2).astype(BF16)
    return w_ext, wuq, wukv, w_out.astype(BF16), ffn_w_in.astype(BF16), ffn_w_out.astype(BF16)


def _pack_vectors(*vectors):
    packed = jnp.concatenate(vectors, axis=-1)[:, None, :]
    blocks, off = [], 0
    for v in vectors:
        w = v.shape[-1]
        assert off % w == 0
        blocks.append((off // w, w))
        off += w
    return lambda layer: [_LayerLanes(packed, layer, blk, w) for blk, w in blocks]


def kernel(x, c, ctx, c_ctx, mod_w, mod_b, pre1_g, post1_g, pre2_g, post2_g, w_in, conv_w, conv_b,
           conv_ln_g, conv_ln_b, ret_log_decay, ret_gn_g, mla_q_norm_g, mla_w_uq, mla_kv_norm_g,
           mla_w_ukv, w_out, ffn_w_in, ffn_w_out):
    B, T, D = x.shape
    TC = ctx.shape[1]
    cv = jnp.concatenate([c, c_ctx[None], jnp.zeros((MOD_ROWS - B - 1, D), F32)], axis=0)
    mod_table = _modulation(cv, mod_w, mod_b).reshape(DEPTH * MOD_ROWS * 6, 1, D)

    tabs_lat = _rope_tables(T, True)
    tabs_ctx = _rope_tables(TC, False)
    zstate = jnp.zeros((B, RET_HEADS // 2, LANES, LANES), F32)

    stacks = _stacked_weights(w_in, mla_w_uq, mla_w_ukv, w_out, ffn_w_in, ffn_w_out)
    stacks += (jnp.pad(conv_w, ((0, 0), (0, 1), (0, 0))),)
    lg_lanes = jnp.repeat(ret_log_decay, RET_DK, axis=-1)
    vectors = _pack_vectors(pre1_g, post1_g, pre2_g, post2_g, mla_q_norm_g, ret_gn_g, lg_lanes[:, 0],
                            lg_lanes[:, 1], conv_b, conv_ln_g, conv_ln_b, mla_kv_norm_g)

    xc = ctx
    for l in range(DEPTH):
        last = l == DEPTH - 1
        w_ext, wuq, wukv, wo, w1, w2, cw = [_Layered(a, l) for a in stacks]
        pre1, post1, pre2, post2, qng, gng, lgf, lgb, cb, clg, clb, kvg = vectors(l)
        lat = _ModRows(mod_table, l * MOD_ROWS * 6, 6)
        cx = _ModRows(mod_table, (l * MOD_ROWS + B) * 6, 0)
        lgd = ret_log_decay[l]

        proj = lambda xx, md, tabs: _inproj(xx, md, pre1, w_ext, qng, wuq, kvg, wukv, tabs)
        aL, rqL, rkL, rvL, rgL, mqL, mkL, mveL, mvoL = proj(x, lat, tabs_lat)
        aC, rqC, rkC, rvC, rgC, mqC, mkC, mveC, mvoC = proj(xc, cx, tabs_ctx)

        yrC, sf, sb = _retention(lgd, rqC, rkC, rvC, rgC, gng, lgf, lgb, zstate, zstate)
        yrL, _, _ = _retention(lgd, rqL, rkL, rvL, rgL, gng, lgf, lgb, sf, sb)

        attL = _attention(mqL, [(mkC, mveC, mvoC), (mkL, mveL, mvoL)])
        mix = lambda a, yr, att, md, xx: _mixer(a, cw, cb, clg, clb, yr, att, wo, post1, md, xx)
        ffn = lambda xx, md: _ffn(xx, md, pre2, w1, w2, post2)
        x = mix(aL, yrL, attL, lat, x)
        if not last:
            attC = _attention(mqC, [(mkC, mveC, mvoC)])
            xc = mix(aC, yrC, attC, cx, xc)
        x = ffn(x, lat)
        if not last:
            xc = ffn(xc, cx)
    return x
```
